```python
import jax, jax.numpy as jnp
from jax import lax
import numpy as np

D_MODEL = 1024
BATCH = 2
SEQ = 8192
DEPTH = 1
DEC_BATCH = 16
DEC_SEQ = 2048
PAST_LEN = 128

PLE_DIM = 256
GRID_W = 64
WIN_R = 8
WIN_C = 16
Q_BLOCK_C = 16
K_BLOCK_C = 32
NA_HEADS = 8
NA_HEAD_DIM = 64
NA_WIDTH = NA_HEADS * NA_HEAD_DIM
RW_HEADS = 8
RW_HEAD_DIM = 64
RW_WIDTH = RW_HEADS * RW_HEAD_DIM
DECAY_RANK = 64
AAA_RANK = 64
GATE_RANK = 128
D_FF = 4 * D_MODEL
N_BRANCH = 2
EPS = 1e-6
GN_EPS = 64e-5
DECAY_SCALE = 0.606531
NEG_INF = -1e30

NA_COLS = 3 * NA_WIDTH
RW_COLS = 3 * RW_WIDTH + 2 * DECAY_RANK + AAA_RANK + GATE_RANK
GATE_COLS = N_BRANCH * D_MODEL
IN_COLS = NA_COLS + RW_COLS + GATE_COLS

kernel_name = "hybrid_natten_rwkv7_bidir_encoder"


def rms_norm(x, g):
    xf = x.astype(jnp.float32)
    y = xf * lax.rsqrt(jnp.mean(xf * xf, axis=-1, keepdims=True) + EPS)
    return (y * g.astype(jnp.float32)).astype(x.dtype)


def neighbourhood_attention(q, k, v, rel_bias):
    B, L, H, dh = q.shape
    rows = L // GRID_W
    kr = min(WIN_R, rows)
    n_cb = GRID_W // Q_BLOCK_C
    nk = kr * K_BLOCK_C
    row_start = np.clip(np.arange(rows) - kr // 2, 0, rows - kr)
    key_rows = row_start[:, None] + np.arange(kr)[None, :]
    blk_start = np.clip(np.arange(n_cb) * Q_BLOCK_C - WIN_C // 2, 0, GRID_W - K_BLOCK_C)
    key_cols = blk_start[:, None] + np.arange(K_BLOCK_C)[None, :]
    idx = (key_rows[:, None, :, None] * GRID_W + key_cols[None, :, None, :]).reshape(rows, n_cb, nk)
    q_cols = np.arange(n_cb)[:, None] * Q_BLOCK_C + np.arange(Q_BLOCK_C)[None, :]
    col_start = np.clip(q_cols - WIN_C // 2, 0, GRID_W - WIN_C)
    in_win = (key_cols[:, None, :] >= col_start[..., None]) & (key_cols[:, None, :] < col_start[..., None] + WIN_C)
    mask = np.broadcast_to(in_win[:, :, None, :], (n_cb, Q_BLOCK_C, kr, K_BLOCK_C))
    dr_idx = key_rows - np.arange(rows)[:, None] + WIN_R - 1
    dc_idx = np.clip(key_cols[:, None, :] - q_cols[..., None] + WIN_C - 1, 0, 2 * WIN_C - 2)
    bias = rel_bias.astype(jnp.float32)[:, dr_idx[:, None, None, :, None], dc_idx[None, :, :, None, :]]
    bias = jnp.where(mask[None, None], bias, NEG_INF).reshape(H, rows, n_cb, Q_BLOCK_C, nk)
    qb = q.reshape(B, rows, n_cb, Q_BLOCK_C, H, dh)
    kg = k[:, idx]
    vg = v[:, idx]
    s = jnp.einsum('brcqhd,brckhd->bhrcqk', qb, kg).astype(jnp.float32) * (dh ** -0.5) + bias[None]
    prob = jax.nn.softmax(s, axis=-1).astype(v.dtype)
    o = jnp.einsum('bhrcqk,brckhd->brcqhd', prob, vg)
    return o.reshape(B, L, H, dh)


def centred_shift(u, w):
    prev = jnp.pad(u[:, :-1], ((0, 0), (1, 0), (0, 0)))
    nxt = jnp.pad(u[:, 1:], ((0, 0), (0, 1), (0, 0)))
    return prev * w[0] + u * w[1] + nxt * w[2]


def rwkv7_bidir(u, w0_f, w_up_f, w0_b, w_up_b, a0, a_up, g_up, k_k, k_a, r_k, ln_x_w, ln_x_b):
    B, L, _ = u.shape
    f32 = lambda t: t.astype(jnp.float32)
    splits = np.cumsum([RW_WIDTH, RW_WIDTH, RW_WIDTH, DECAY_RANK, DECAY_RANK, AAA_RANK])
    r, k, v, dwf, dwb, da, dg = jnp.split(u, splits, axis=-1)
    decay_f = jnp.exp(-DECAY_SCALE * jax.nn.sigmoid(f32(w0_f + jnp.tanh(dwf) @ w_up_f)))
    decay_b = jnp.exp(-DECAY_SCALE * jax.nn.sigmoid(f32(w0_b + jnp.tanh(dwb) @ w_up_b)))
    a = jax.nn.sigmoid(f32(a0 + da @ a_up))
    g = jax.nn.sigmoid(dg) @ g_up
    heads = lambda t: t.reshape(B, L, RW_HEADS, RW_HEAD_DIM)
    kk = heads(f32(k * k_k))
    kk = kk / jnp.maximum(jnp.sqrt(jnp.sum(kk * kk, axis=-1, keepdims=True)), 1e-12)
    kt = f32(k) * (1.0 + (a - 1.0) * f32(k_a))
    rh, kh, vh, ah = heads(f32(r)), heads(kt), heads(f32(v)), heads(a)

    def step(S, inp):
        r_t, w_t, k_t, v_t, kk_t, a_t = inp
        sa = jnp.einsum('bhvk,bhk->bhv', S, -kk_t)
        S = S * w_t[:, :, None, :] + sa[..., None] * (kk_t * a_t)[:, :, None, :] + v_t[..., None] * k_t[:, :, None, :]
        return S, jnp.einsum('bhvk,bhk->bhv', S, r_t)

    tm = lambda t: jnp.moveaxis(t, 1, 0)
    S0 = jnp.zeros((B, RW_HEADS, RW_HEAD_DIM, RW_HEAD_DIM), jnp.float32)
    r_tm, k_tm, v_tm, kk_tm, a_tm = tm(rh), tm(kh), tm(vh), tm(kk), tm(ah)
    _, y_f = lax.scan(step, S0, (r_tm, tm(heads(decay_f)), k_tm, v_tm, kk_tm, a_tm))
    _, y_b = lax.scan(step, S0, (r_tm, tm(heads(decay_b)), k_tm, v_tm, kk_tm, a_tm), reverse=True)
    o = jnp.moveaxis(y_f + y_b, 0, 1)
    mu = jnp.mean(o, axis=-1, keepdims=True)
    var = jnp.mean(jnp.square(o - mu), axis=-1, keepdims=True)
    o = ((o - mu) * lax.rsqrt(var + GN_EPS)).reshape(B, L, RW_WIDTH) * f32(ln_x_w) + f32(ln_x_b)
    bonus = jnp.sum(rh * kh * f32(r_k), axis=-1, keepdims=True) * vh
    out = (o + bonus.reshape(B, L, RW_WIDTH)) * f32(g)
    return out.astype(u.dtype)


def encoder_layer(x, p, g_mix, w_in, conv_w, q_gain, k_gain, rel_bias, w0_f, w_up_f, w0_b, w_up_b,
                  a0, a_up, g_up, k_k, k_a, r_k, ln_x_w, ln_x_b, w_a_out, w_b_out, w_o,
                  g_ffn, w_ff1, w_ff2, w_ple, g_ple, w_pgate):
    B, L, _ = x.shape
    n = rms_norm(x, g_mix)
    z = n @ w_in
    z_na, z_rw, z_gate = jnp.split(z, [NA_COLS, NA_COLS + RW_COLS], axis=-1)
    q, k, v = [t.reshape(B, L, NA_HEADS, NA_HEAD_DIM) for t in jnp.split(z_na, 3, axis=-1)]
    q = rms_norm(q, q_gain)
    k = rms_norm(k, k_gain)
    y_a = neighbourhood_attention(q, k, v, rel_bias).reshape(B, L, NA_WIDTH)
    y_b = rwkv7_bidir(centred_shift(z_rw, conv_w), w0_f, w_up_f, w0_b, w_up_b, a0, a_up, g_up,
                      k_k, k_a, r_k, ln_x_w, ln_x_b)
    gate_a, gate_b = jnp.split(z_gate, N_BRANCH, axis=-1)
    merged = jax.nn.sigmoid(gate_a) * (y_a @ w_a_out) + jax.nn.sigmoid(gate_b) * (y_b @ w_b_out)
    h = x + merged @ w_o
    h = h + jnp.square(jax.nn.relu(rms_norm(h, g_ffn) @ w_ff1)) @ w_ff2
    h = h + jax.nn.sigmoid(rms_norm(h, g_ple) @ w_pgate) * (p @ w_ple)
    return h


def setup_inputs(seed: int = 0) -> dict:
    key = jax.random.key(seed)
    ks = jax.random.split(key, 32)
    nrm = lambda k, shape, scale: jax.random.normal(k, shape, jnp.float32) * scale
    conv_base = jnp.array([0.25, 0.5, 0.25], jnp.float32)[None, :, None]
    return {
        'x_prompt': nrm(ks[0], (BATCH, SEQ, D_MODEL), 1.0),
        'x_sample': nrm(ks[1], (DEC_BATCH, DEC_SEQ, D_MODEL), 1.0),
        'p_prompt': nrm(ks[2], (DEPTH, BATCH, SEQ, PLE_DIM), 1.0),
        'p_sample': nrm(ks[3], (DEPTH, DEC_BATCH, DEC_SEQ, PLE_DIM), 1.0),
        'g_mix': 1.0 + nrm(ks[4], (DEPTH, D_MODEL), 0.02),
        'w_in': nrm(ks[5], (DEPTH, D_MODEL, IN_COLS), D_MODEL ** -0.5),
        'conv_w': conv_base + nrm(ks[6], (DEPTH, 3, RW_COLS), 0.05),
        'q_gain': 1.0 + nrm(ks[7], (DEPTH, NA_HEAD_DIM), 0.02),
        'k_gain': 1.0 + nrm(ks[8], (DEPTH, NA_HEAD_DIM), 0.02),
        'rel_bias': nrm(ks[9], (DEPTH, NA_HEADS, 2 * WIN_R - 1, 2 * WIN_C - 1), 0.02),
        'w0_f': -1.0 + nrm(ks[10], (DEPTH, RW_WIDTH), 0.5),
        'w_up_f': nrm(ks[11], (DEPTH, DECAY_RANK, RW_WIDTH), 0.1 * DECAY_RANK ** -0.5),
        'w0_b': -1.0 + nrm(ks[12], (DEPTH, RW_WIDTH), 0.5),
        'w_up_b': nrm(ks[13], (DEPTH, DECAY_RANK, RW_WIDTH), 0.1 * DECAY_RANK ** -0.5),
        'a0': nrm(ks[14], (DEPTH, RW_WIDTH), 0.1),
        'a_up': nrm(ks[15], (DEPTH, AAA_RANK, RW_WIDTH), 0.1 * AAA_RANK ** -0.5),
        'g_up': nrm(ks[16], (DEPTH, GATE_RANK, RW_WIDTH), GATE_RANK ** -0.5),
        'k_k': 0.85 + nrm(ks[17], (DEPTH, RW_WIDTH), 0.02),
        'k_a': 1.0 + nrm(ks[18], (DEPTH, RW_WIDTH), 0.02),
        'r_k': nrm(ks[19], (DEPTH, RW_HEADS, RW_HEAD_DIM), 0.1),
        'ln_x_w': 1.0 + nrm(ks[20], (DEPTH, RW_WIDTH), 0.02),
        'ln_x_b': nrm(ks[21], (DEPTH, RW_WIDTH), 0.01),
        'w_a_out': nrm(ks[22], (DEPTH, NA_WIDTH, D_MODEL), NA_WIDTH ** -0.5),
        'w_b_out': nrm(ks[23], (DEPTH, RW_WIDTH, D_MODEL), RW_WIDTH ** -0.5),
        'w_o': nrm(ks[24], (DEPTH, D_MODEL, D_MODEL), D_MODEL ** -0.5),
        'g_ffn': 1.0 + nrm(ks[25], (DEPTH, D_MODEL), 0.02),
        'w_ff1': nrm(ks[26], (DEPTH, D_MODEL, D_FF), D_MODEL ** -0.5),
        'w_ff2': nrm(ks[27], (DEPTH, D_FF, D_MODEL), D_FF ** -0.5),
        'w_ple': nrm(ks[28], (DEPTH, PLE_DIM, D_MODEL), PLE_DIM ** -0.5),
        'g_ple': 1.0 + nrm(ks[29], (DEPTH, D_MODEL), 0.02),
        'w_pgate': nrm(ks[30], (DEPTH, D_MODEL, D_MODEL), D_MODEL ** -0.5),
    }


def reference(x_prompt, x_sample, p_prompt, p_sample, g_mix, w_in, conv_w, q_gain, k_gain, rel_bias,
              w0_f, w_up_f, w0_b, w_up_b, a0, a_up, g_up, k_k, k_a, r_k, ln_x_w, ln_x_b,
              w_a_out, w_b_out, w_o, g_ffn, w_ff1, w_ff2, w_ple, g_ple, w_pgate):
    def run(h, p):
        for i in range(DEPTH):
            h = encoder_layer(h, p[i], g_mix[i], w_in[i], conv_w[i], q_gain[i], k_gain[i], rel_bias[i],
                              w0_f[i], w_up_f[i], w0_b[i], w_up_b[i], a0[i], a_up[i], g_up[i],
                              k_k[i], k_a[i], r_k[i], ln_x_w[i], ln_x_b[i], w_a_out[i], w_b_out[i], w_o[i],
                              g_ffn[i], w_ff1[i], w_ff2[i], w_ple[i], g_ple[i], w_pgate[i])
        return h
    y_prompt = run(x_prompt, p_prompt)
    y_sample = run(x_sample, p_sample)
    return (y_prompt, y_sample)
```

```python
import functools

import numpy as np
import jax
import jax.numpy as jnp
from jax import lax
from jax.experimental import pallas as pl
from jax.experimental.pallas import tpu as pltpu

F32 = jnp.float32
BF16 = jnp.bfloat16

D_MODEL = 1024
PLE_DIM = 256
GRID_W = 64
WIN_R = 8
WIN_C = 16
HEADS = 8
HEAD_DIM = 64
WIDTH = HEADS * HEAD_DIM
DECAY_RANK = 64
AAA_RANK = 64
GATE_RANK = 128
D_FF = 4 * D_MODEL
EPS = 1e-6
GN_EPS = 64e-5
DECAY_SCALE = 0.606531
NEG_INF = -1e30

SMALL_COLS = 384
IN_COLS_PADDED = 3 * WIDTH + 3 * WIDTH + SMALL_COLS + 2 * D_MODEL
CHUNK = 64
NA_ROWS_PER_STEP = 8
VMEM_LIMIT = 56 * 1024 * 1024


def _dot(a, b):
    return jnp.dot(a, b, preferred_element_type=F32)


def _dot_nt(a, b):
    return lax.dot_general(a, b, (((1,), (1,)), ((), ())), preferred_element_type=F32)


def _dot_tn(a, b):
    return lax.dot_general(a, b, (((0,), (0,)), ((), ())), preferred_element_type=F32)


def _split_dot(x, m):
    hi = x.astype(BF16)
    lo = (x - hi.astype(F32)).astype(BF16)
    return _dot(hi, m) + _dot(lo, m)


def _rms(x, gain):
    return x * lax.rsqrt(jnp.mean(x * x, axis=-1, keepdims=True) + EPS) * gain


def _inproj_kernel(x_ref, gmix_ref, w_ref, qg_ref, kg_ref, seg_ref,
                   qkv_ref, rkv_ref, small_ref, gate_ref):
    n = _rms(x_ref[...], gmix_ref[...]).astype(BF16)
    seg = seg_ref[...]

    def head_rms(t, gain):
        ss = _split_dot(t * t, seg)
        return t * lax.rsqrt(ss * (1.0 / HEAD_DIM) + EPS) * gain

    q = head_rms(_dot(n, w_ref[:, 0:WIDTH]), qg_ref[...]) * (HEAD_DIM ** -0.5)
    qkv_ref[:, 0:WIDTH] = q.astype(BF16)
    k = head_rms(_dot(n, w_ref[:, WIDTH:2 * WIDTH]), kg_ref[...])
    qkv_ref[:, WIDTH:2 * WIDTH] = k.astype(BF16)
    qkv_ref[:, 2 * WIDTH:3 * WIDTH] = _dot(n, w_ref[:, 2 * WIDTH:3 * WIDTH]).astype(BF16)
    c0 = 3 * WIDTH
    rkv_ref[...] = _dot(n, w_ref[:, c0:c0 + 3 * WIDTH])
    c0 += 3 * WIDTH
    small_ref[...] = _dot(n, w_ref[:, c0:c0 + SMALL_COLS])
    c0 += SMALL_COLS
    gate_ref[...] = jax.nn.sigmoid(_dot(n, w_ref[:, c0:c0 + 2 * D_MODEL]))


def _inproj(x2, gmix, w_in_p, qg, kg, seg, tm):
    m = x2.shape[0]
    const = lambda i: (0, 0)
    row = lambda i: (i, 0)
    return pl.pallas_call(
        _inproj_kernel,
        grid=(m // tm,),
        in_specs=[
            pl.BlockSpec((tm, D_MODEL), row),
            pl.BlockSpec((1, D_MODEL), const),
            pl.BlockSpec((D_MODEL, IN_COLS_PADDED), const, pipeline_mode=pl.Buffered(1)),
            pl.BlockSpec((1, WIDTH), const),
            pl.BlockSpec((1, WIDTH), const),
            pl.BlockSpec((WIDTH, WIDTH), const),
        ],
        out_specs=[
            pl.BlockSpec((tm, 3 * WIDTH), row),
            pl.BlockSpec((tm, 3 * WIDTH), row),
            pl.BlockSpec((tm, SMALL_COLS), row),
            pl.BlockSpec((tm, 2 * D_MODEL), row),
        ],
        out_shape=[
            jax.ShapeDtypeStruct((m, 3 * WIDTH), BF16),
            jax.ShapeDtypeStruct((m, 3 * WIDTH), F32),
            jax.ShapeDtypeStruct((m, SMALL_COLS), F32),
            jax.ShapeDtypeStruct((m, 2 * D_MODEL), F32),
        ],
        compiler_params=pltpu.CompilerParams(
            dimension_semantics=("arbitrary",), vmem_limit_bytes=VMEM_LIMIT),
        name="inproj",
    )(x2, gmix, w_in_p, qg, kg, seg)


def _natten_kernel(q_ref, k_ref, v_ref, bias_ref, o_ref, *, rows):
    r0 = pl.program_id(2) * NA_ROWS_PER_STEP
    win = WIN_R * GRID_W

    def body(i, carry):
        r = r0 + i
        row_start = jnp.clip(r - WIN_R // 2, 0, rows - WIN_R)
        off = r - row_start
        kstart = pl.multiple_of(row_start * GRID_W, GRID_W)
        qstart = pl.multiple_of(i * GRID_W, GRID_W)
        kw = k_ref[0, pl.ds(kstart, win), :]
        vw = v_ref[0, pl.ds(kstart, win), :]
        qrow = q_ref[0, pl.ds(qstart, GRID_W), :]
        outs = []
        for hh in range(2):
            sl = slice(hh * HEAD_DIM, (hh + 1) * HEAD_DIM)
            s = _dot_nt(qrow[:, sl], kw[:, sl]) + bias_ref[0, hh, off]
            e = jnp.exp(s - jnp.max(s, axis=-1, keepdims=True))
            l = jnp.sum(e, axis=-1, keepdims=True)
            outs.append(_dot(e.astype(BF16), vw[:, sl]) / l)
        o_ref[0, pl.ds(qstart, GRID_W), :] = jnp.concatenate(outs, axis=1).astype(BF16)
        return carry

    lax.fori_loop(0, NA_ROWS_PER_STEP, body, 0)


def _natten(qkv3, bias_tab):
    b, l, _ = qkv3.shape
    rows = l // GRID_W
    assert rows >= WIN_R and rows % NA_ROWS_PER_STEP == 0
    tq = NA_ROWS_PER_STEP * GRID_W
    npair = HEADS // 2
    return pl.pallas_call(
        functools.partial(_natten_kernel, rows=rows),
        grid=(b, npair, rows // NA_ROWS_PER_STEP),
        in_specs=[
            pl.BlockSpec((1, tq, 128), lambda bi, hp, ri: (bi, ri, hp)),
            pl.BlockSpec((1, l, 128), lambda bi, hp, ri: (bi, 0, npair + hp)),
            pl.BlockSpec((1, l, 128), lambda bi, hp, ri: (bi, 0, 2 * npair + hp)),
            pl.BlockSpec((1, 2, WIN_R, GRID_W, WIN_R * GRID_W), lambda bi, hp, ri: (hp, 0, 0, 0, 0)),
        ],
        out_specs=pl.BlockSpec((1, tq, 128), lambda bi, hp, ri: (bi, ri, hp)),
        out_shape=jax.ShapeDtypeStruct((b, l, WIDTH), BF16),
        compiler_params=pltpu.CompilerParams(
            dimension_semantics=("arbitrary", "arbitrary", "arbitrary"), vmem_limit_bytes=VMEM_LIMIT),
        name="natten",
    )(qkv3, qkv3, qkv3, bias_tab)


def _na_bias_table(rel_bias):
    off = np.arange(WIN_R)[:, None]
    j = np.arange(WIN_R)[None, :]
    dr = j - off + WIN_R - 1
    qc = np.arange(GRID_W)[:, None]
    kc = np.arange(GRID_W)[None, :]
    dc = np.clip(kc - qc + WIN_C - 1, 0, 2 * WIN_C - 2)
    col_start = np.clip(qc - WIN_C // 2, 0, GRID_W - WIN_C)
    in_win = (kc >= col_start) & (kc < col_start + WIN_C)
    tab = rel_bias.astype(F32)[:, dr[:, None, :, None], dc[None, :, None, :]]
    tab = jnp.where(in_win[None, None, :, None, :], tab, NEG_INF)
    return tab.reshape(HEADS // 2, 2, WIN_R, GRID_W, WIN_R * GRID_W)


def _rwprep_kernel(rkv_ref, rkv_prev_ref, rkv_next_ref, sm_ref, sm_prev_ref, sm_next_ref,
                   cw_rkv_ref, cw_sm_ref, wup_ref, w0_ref, aup_ref, a0_ref, gup_ref,
                   kk_ref, ka_ref, seg_ref,
                   r_out, kt_out, v_out, al_out, be_out, lwf_out, lwb_out, g_out, *, tm):
    i = pl.program_id(1)
    last = pl.num_programs(1) - 1

    def shifted(z, prev8, next8, cw):
        prow = jnp.where(i > 0, prev8[7:8, :], 0.0)
        nrow = jnp.where(i < last, next8[0:1, :], 0.0)
        rowid = lax.broadcasted_iota(jnp.int32, z.shape, 0)
        zp = jnp.where(rowid == 0, prow, pltpu.roll(z, 1, 0))
        zn = jnp.where(rowid == tm - 1, nrow, pltpu.roll(z, tm - 1, 0))
        return zp * cw[0:1, :] + z * cw[1:2, :] + zn * cw[2:3, :]

    u = shifted(rkv_ref[0], rkv_prev_ref[0], rkv_next_ref[0], cw_rkv_ref[...])
    us = shifted(sm_ref[0], sm_prev_ref[0], sm_next_ref[0], cw_sm_ref[...])
    r = u[:, 0:WIDTH]
    k = u[:, WIDTH:2 * WIDTH]
    v = u[:, 2 * WIDTH:3 * WIDTH]
    pre = _dot(jnp.tanh(us[:, 0:128]).astype(BF16), wup_ref[...]) + w0_ref[...]
    logw = -DECAY_SCALE * jax.nn.sigmoid(pre)
    a = jax.nn.sigmoid(_dot(us[:, 128:256].astype(BF16), aup_ref[...]) + a0_ref[...])
    g = _dot(jax.nn.sigmoid(us[:, 256:384]).astype(BF16), gup_ref[...])
    kk = k * kk_ref[...]
    ss = _split_dot(kk * kk, seg_ref[...])
    kk = kk / jnp.maximum(jnp.sqrt(ss), 1e-12)
    r_out[0] = r
    kt_out[0] = k * (1.0 + (a - 1.0) * ka_ref[...])
    v_out[0] = v
    al_out[0] = -kk
    be_out[0] = kk * a
    lwf_out[0] = logw[:, 0:WIDTH]
    lwb_out[0] = logw[:, WIDTH:2 * WIDTH]
    g_out[0] = g


def _rwprep(rkv3, sm3, wts, tm):
    b, l, _ = rkv3.shape
    nb8 = l // 8
    step8 = tm // 8
    cur = lambda bi, i: (bi, i, 0)
    prev = lambda bi, i: (bi, jnp.maximum(i * step8 - 1, 0), 0)
    nxt = lambda bi, i: (bi, jnp.minimum((i + 1) * step8, nb8 - 1), 0)
    const = lambda bi, i: (0, 0)
    wide = lambda c: pl.BlockSpec((1, tm, c), cur)
    out = jax.ShapeDtypeStruct((b, l, WIDTH), F32)
    return pl.pallas_call(
        functools.partial(_rwprep_kernel, tm=tm),
        grid=(b, l // tm),
        in_specs=[
            wide(3 * WIDTH), pl.BlockSpec((1, 8, 3 * WIDTH), prev), pl.BlockSpec((1, 8, 3 * WIDTH), nxt),
            wide(SMALL_COLS), pl.BlockSpec((1, 8, SMALL_COLS), prev), pl.BlockSpec((1, 8, SMALL_COLS), nxt),
            pl.BlockSpec((3, 3 * WIDTH), const), pl.BlockSpec((3, SMALL_COLS), const),
            pl.BlockSpec((128, 2 * WIDTH), const), pl.BlockSpec((1, 2 * WIDTH), const),
            pl.BlockSpec((128, WIDTH), const), pl.BlockSpec((1, WIDTH), const),
            pl.BlockSpec((128, WIDTH), const),
            pl.BlockSpec((1, WIDTH), const), pl.BlockSpec((1, WIDTH), const),
            pl.BlockSpec((WIDTH, WIDTH), const),
        ],
        out_specs=[wide(WIDTH)] * 8,
        out_shape=[out] * 8,
        compiler_params=pltpu.CompilerParams(
            dimension_semantics=("arbitrary", "arbitrary"), vmem_limit_bytes=VMEM_LIMIT),
        name="rwprep",
    )(rkv3, rkv3, rkv3, sm3, sm3, sm3, *wts)


def _scan_direction(refs, y_ref, s_ref, d, reverse):
    r_ref, k_ref, v_ref, al_ref, be_ref, lw_ref = refs
    c = CHUNK
    ii = lax.broadcasted_iota(jnp.int32, (c, c), 0)
    jj = lax.broadcasted_iota(jnp.int32, (c, c), 1)
    tri = ((ii <= jj) if reverse else (ii >= jj)).astype(BF16)
    i2 = lax.broadcasted_iota(jnp.int32, (2 * c, 2 * c), 0)
    j2 = lax.broadcasted_iota(jnp.int32, (2 * c, 2 * c), 1) & (c - 1)
    lag = (j2 - (i2 & (c - 1))) if reverse else ((i2 & (c - 1)) - j2)
    mask = lag >= jnp.where(i2 < c, 1, 0)
    lw = lw_ref[0]
    lw_hi = lw.astype(BF16)
    lw_lo = (lw - lw_hi.astype(F32)).astype(BF16)
    g = _dot(tri, lw_hi) + _dot(tri, lw_lo)
    gex = g - lw
    gtot = g[0:1, :] if reverse else g[c - 1:c, :]
    eng = jnp.exp(-g)
    egt = jnp.exp(gtot - g)
    al = al_ref[0]
    be = be_ref[0]
    kt = k_ref[0]
    v = v_ref[0]
    a_dec = al * jnp.exp(gex)
    r_dec = r_ref[0] * jnp.exp(g)
    x_all = jnp.concatenate([a_dec, r_dec], axis=0).astype(BF16)
    y_all = jnp.concatenate([be * eng, kt * eng], axis=0).astype(BF16)
    yh_all = jnp.concatenate([be * egt, kt * egt], axis=0).astype(BF16)
    dec_tot = jnp.exp(gtot)
    outs = []
    for h in range(HEADS):
        sl = slice(h * HEAD_DIM, (h + 1) * HEAD_DIM)
        sc = jnp.where(mask, _dot_nt(x_all[:, sl], y_all[:, sl]), 0.0)
        a_ab = sc[0:c, 0:c]
        a_ak = sc[0:c, c:2 * c]
        a_r = sc[c:2 * c, :]
        vh = v[:, sl]
        z = jnp.concatenate([a_dec[:, sl], _dot(a_ak.astype(BF16), vh.astype(BF16))], axis=1)
        p = a_ab
        for step in range(6):
            pb = p.astype(BF16)
            if step < 5:
                w = _dot(pb, jnp.concatenate([z, p], axis=1).astype(BF16))
                z = z + w[:, 0:2 * HEAD_DIM]
                p = w[:, 2 * HEAD_DIM:]
            else:
                z = z + _dot(pb, z.astype(BF16))
        s_old = s_ref[d, h]
        sb = s_old.astype(BF16)
        u = _dot_nt(z[:, 0:HEAD_DIM].astype(BF16), sb) + z[:, HEAD_DIM:]
        uv = jnp.concatenate([u, vh], axis=0).astype(BF16)
        outs.append(_dot_nt(x_all[c:2 * c, sl], sb) + _dot(a_r.astype(BF16), uv))
        s_ref[d, h] = s_old * dec_tot[:, sl] + _dot_tn(uv, yh_all[:, sl])
    y_ref[0] = jnp.concatenate(outs, axis=1)


def _rwscan_kernel(rf, kf, vf, af, bf, lwf, rb, kb, vb, ab, bb, lwb, yf_ref, yb_ref, s_ref):
    @pl.when(pl.program_id(1) == 0)
    def _():
        s_ref[...] = jnp.zeros_like(s_ref)

    _scan_direction((rf, kf, vf, af, bf, lwf), yf_ref, s_ref, 0, False)
    _scan_direction((rb, kb, vb, ab, bb, lwb), yb_ref, s_ref, 1, True)


def _rwscan(r, kt, v, al, be, lwf, lwb):
    b, l, _ = r.shape
    nc = l // CHUNK
    fwd = lambda bi, n: (bi, n, 0)
    bwd = lambda bi, n: (bi, nc - 1 - n, 0)
    spec = lambda m: pl.BlockSpec((1, CHUNK, WIDTH), m)
    out = jax.ShapeDtypeStruct((b, l, WIDTH), F32)
    return pl.pallas_call(
        _rwscan_kernel,
        grid=(b, nc),
        in_specs=[spec(fwd)] * 6 + [spec(bwd)] * 6,
        out_specs=[spec(fwd), spec(bwd)],
        out_shape=[out, out],
        scratch_shapes=[pltpu.VMEM((2, HEADS, HEAD_DIM, HEAD_DIM), F32)],
        compiler_params=pltpu.CompilerParams(
            dimension_semantics=("arbitrary", "arbitrary"), vmem_limit_bytes=VMEM_LIMIT),
        name="rwscan",
    )(r, kt, v, al, be, lwf, r, kt, v, al, be, lwb)


def _merge_kernel(x_ref, ya_ref, yf_ref, yb_ref, r_ref, kt_ref, v_ref, g_ref, gate_ref,
                  lnw_ref, lnb_ref, rk_ref, seg_ref, wa_ref, wb_ref, wo_ref, h_ref):
    seg = seg_ref[...]
    o = yf_ref[...] + yb_ref[...]
    mu = _split_dot(o, seg) * (1.0 / HEAD_DIM)
    dlt = o - mu
    var = _split_dot(dlt * dlt, seg) * (1.0 / HEAD_DIM)
    o = dlt * lax.rsqrt(var + GN_EPS) * lnw_ref[...] + lnb_ref[...]
    v = v_ref[...]
    bonus = _split_dot(r_ref[...] * kt_ref[...] * rk_ref[...], seg) * v
    y_rw = ((o + bonus) * g_ref[...]).astype(BF16)
    merged = (gate_ref[:, 0:D_MODEL] * _dot(ya_ref[...], wa_ref[...])
              + gate_ref[:, D_MODEL:2 * D_MODEL] * _dot(y_rw, wb_ref[...]))
    h_ref[...] = x_ref[...] + _dot(merged.astype(BF16), wo_ref[...])


def _merge(x2, ya2, yf2, yb2, r2, kt2, v2, g2, gate2, lnw, lnb, rk, seg, wa, wb, wo, tm):
    m = x2.shape[0]
    const = lambda i: (0, 0)
    row = lambda i: (i, 0)
    rw = pl.BlockSpec((tm, WIDTH), row)
    vec = pl.BlockSpec((1, WIDTH), const)
    return pl.pallas_call(
        _merge_kernel,
        grid=(m // tm,),
        in_specs=[
            pl.BlockSpec((tm, D_MODEL), row), rw, rw, rw, rw, rw, rw, rw,
            pl.BlockSpec((tm, 2 * D_MODEL), row),
            vec, vec, vec,
            pl.BlockSpec((WIDTH, WIDTH), const),
            pl.BlockSpec((WIDTH, D_MODEL), const),
            pl.BlockSpec((WIDTH, D_MODEL), const),
            pl.BlockSpec((D_MODEL, D_MODEL), const),
        ],
        out_specs=pl.BlockSpec((tm, D_MODEL), row),
        out_shape=jax.ShapeDtypeStruct((m, D_MODEL), F32),
        compiler_params=pltpu.CompilerParams(
            dimension_semantics=("arbitrary",), vmem_limit_bytes=VMEM_LIMIT),
        name="merge",
    )(x2, ya2, yf2, yb2, r2, kt2, v2, g2, gate2, lnw, lnb, rk, seg, wa, wb, wo)


def _ffn_kernel(h_ref, p_ref, gffn_ref, w1_ref, w2_ref, gple_ref, wpg_ref, wple_ref, o_ref, *, ff_chunk):
    h = h_ref[...]
    n = _rms(h, gffn_ref[...]).astype(BF16)
    acc = h
    for c0 in range(0, D_FF, ff_chunk):
        f = jnp.square(jnp.maximum(_dot(n, w1_ref[:, c0:c0 + ff_chunk]), 0.0))
        acc = acc + _dot(f.astype(BF16), w2_ref[c0:c0 + ff_chunk, :])
    n2 = _rms(acc, gple_ref[...]).astype(BF16)
    pg = jax.nn.sigmoid(_dot(n2, wpg_ref[...]))
    o_ref[...] = acc + pg * _dot(p_ref[...].astype(BF16), wple_ref[...])


def _ffn(h2, p2, gffn, w1, w2, gple, wpg, wple, tm):
    m = h2.shape[0]
    const = lambda i: (0, 0)
    row = lambda i: (i, 0)
    once = lambda shape: pl.BlockSpec(shape, const, pipeline_mode=pl.Buffered(1))
    return pl.pallas_call(
        functools.partial(_ffn_kernel, ff_chunk=512),
        grid=(m // tm,),
        in_specs=[
            pl.BlockSpec((tm, D_MODEL), row),
            pl.BlockSpec((tm, PLE_DIM), row),
            pl.BlockSpec((1, D_MODEL), const),
            once((D_MODEL, D_FF)), once((D_FF, D_MODEL)),
            pl.BlockSpec((1, D_MODEL), const),
            once((D_MODEL, D_MODEL)), once((PLE_DIM, D_MODEL)),
        ],
        out_specs=pl.BlockSpec((tm, D_MODEL), row),
        out_shape=jax.ShapeDtypeStruct((m, D_MODEL), F32),
        compiler_params=pltpu.CompilerParams(
            dimension_semantics=("arbitrary",), vmem_limit_bytes=VMEM_LIMIT),
        name="ffn",
    )(h2, p2, gffn, w1, w2, gple, wpg, wple)


def _prepare_weights(g_mix, w_in, conv_w, q_gain, k_gain, rel_bias, w0_f, w_up_f, w0_b, w_up_b,
                     a0, a_up, g_up, k_k, k_a, r_k, ln_x_w, ln_x_b, w_a_out, w_b_out, w_o,
                     g_ffn, w_ff1, w_ff2, w_ple, g_ple, w_pgate):
    na = 3 * WIDTH
    rw0 = na
    sm0 = rw0 + 3 * WIDTH
    gate0 = sm0 + 2 * DECAY_RANK + AAA_RANK + GATE_RANK

    def regroup_small(t):
        lead = t[..., 0:2 * DECAY_RANK + AAA_RANK]
        pad = jnp.zeros(t.shape[:-1] + (128 - AAA_RANK,), t.dtype)
        return jnp.concatenate([lead, pad, t[..., 2 * DECAY_RANK + AAA_RANK:]], axis=-1)

    w_in_p = jnp.concatenate(
        [w_in[:, 0:sm0], regroup_small(w_in[:, sm0:gate0]), w_in[:, gate0:]], axis=1).astype(BF16)
    conv_rw = conv_w[:, 0:3 * WIDTH]
    conv_sm = regroup_small(conv_w[:, 3 * WIDTH:])
    zeros_up = jnp.zeros((DECAY_RANK, WIDTH), F32)
    w_up = jnp.concatenate([jnp.concatenate([w_up_f, zeros_up], axis=1),
                            jnp.concatenate([zeros_up, w_up_b], axis=1)], axis=0).astype(BF16)
    w0 = jnp.concatenate([w0_f, w0_b])[None, :]
    a_up_p = jnp.concatenate([a_up, jnp.zeros((128 - AAA_RANK, WIDTH), F32)], axis=0).astype(BF16)
    head_id = np.arange(WIDTH) // HEAD_DIM
    seg = jnp.asarray(head_id[:, None] == head_id[None, :], BF16)
    row = lambda t: t.reshape(1, -1)
    return dict(
        g_mix=row(g_mix), w_in=w_in_p, qg=row(jnp.tile(q_gain, HEADS)), kg=row(jnp.tile(k_gain, HEADS)),
        seg=seg, bias=_na_bias_table(rel_bias),
        prep=(conv_rw, conv_sm, w_up, w0, a_up_p, row(a0), g_up.astype(BF16), row(k_k), row(k_a), seg),
        lnw=row(ln_x_w), lnb=row(ln_x_b), rk=row(r_k),
        wa=w_a_out.astype(BF16), wb=w_b_out.astype(BF16), wo=w_o.astype(BF16),
        g_ffn=row(g_ffn), w1=w_ff1.astype(BF16), w2=w_ff2.astype(BF16),
        g_ple=row(g_ple), wpg=w_pgate.astype(BF16), wple=w_ple.astype(BF16),
    )


def _layer(x, p, w):
    b, l, _ = x.shape
    m = b * l
    tm = 512
    x2 = x.reshape(m, D_MODEL)
    qkv, rkv, small, gate = _inproj(x2, w["g_mix"], w["w_in"], w["qg"], w["kg"], w["seg"], tm)
    y_a = _natten(qkv.reshape(b, l, 3 * WIDTH), w["bias"])
    r, kt, v, al, be, lwf, lwb, g = _rwprep(
        rkv.reshape(b, l, 3 * WIDTH), small.reshape(b, l, SMALL_COLS), w["prep"], 256)
    y_f, y_b = _rwscan(r, kt, v, al, be, lwf, lwb)
    flat = lambda t: t.reshape(m, WIDTH)
    h = _merge(x2, flat(y_a), flat(y_f), flat(y_b), flat(r), flat(kt), flat(v), flat(g), gate,
               w["lnw"], w["lnb"], w["rk"], w["seg"], w["wa"], w["wb"], w["wo"], tm)
    out = _ffn(h, p.reshape(m, PLE_DIM), w["g_ffn"], w["w1"], w["w2"], w["g_ple"], w["wpg"], w["wple"], tm)
    return out.reshape(b, l, D_MODEL)


def kernel(x_prompt, x_sample, p_prompt, p_sample, g_mix, w_in, conv_w, q_gain, k_gain, rel_bias,
           w0_f, w_up_f, w0_b, w_up_b, a0, a_up, g_up, k_k, k_a, r_k, ln_x_w, ln_x_b,
           w_a_out, w_b_out, w_o, g_ffn, w_ff1, w_ff2, w_ple, g_ple, w_pgate):
    params = (g_mix, w_in, conv_w, q_gain, k_gain, rel_bias, w0_f, w_up_f, w0_b, w_up_b, a0, a_up, g_up,
              k_k, k_a, r_k, ln_x_w, ln_x_b, w_a_out, w_b_out, w_o, g_ffn, w_ff1, w_ff2, w_ple, g_ple, w_pgate)
    depth = g_mix.shape[0]

    def run(h, p):
        for i in range(depth):
            h = _layer(h, p[i], _prepare_weights(*[t[i] for t in params]))
        return h

    return (run(x_prompt, p_prompt), run(x_sample, p_sample))
```

```python
import functools

import numpy as np
import jax
import jax.numpy as jnp
from jax import lax
from jax.experimental import pallas as pl
from jax.experimental.pallas import tpu as pltpu

F32 = jnp.float32
BF16 = jnp.bfloat16

D_MODEL = 1024
PLE_DIM = 256
GRID_W = 64
WIN_R = 8
WIN_C = 16
HEADS = 8
HEAD_DIM = 64
WIDTH = HEADS * HEAD_DIM
DECAY_RANK = 64
AAA_RANK = 64
GATE_RANK = 128
D_FF = 4 * D_MODEL
EPS = 1e-6
GN_EPS = 64e-5
DECAY_SCALE = 0.606531
NEG_INF = -1e30

SMALL_COLS = 384
IN_COLS_PADDED = 3 * WIDTH + 3 * WIDTH + SMALL_COLS + 2 * D_MODEL
CHUNK = 64
NA_ROWS_PER_STEP = 8
VMEM_LIMIT = 56 * 1024 * 1024


def _dot(a, b):
    return jnp.dot(a, b, preferred_element_type=F32)


def _dot_nt(a, b):
    return lax.dot_general(a, b, (((1,), (1,)), ((), ())), preferred_element_type=F32)


def _dot_tn(a, b):
    return lax.dot_general(a, b, (((0,), (0,)), ((), ())), preferred_element_type=F32)


def _split_dot(x, m):
    hi = x.astype(BF16)
    lo = (x - hi.astype(F32)).astype(BF16)
    return _dot(hi, m) + _dot(lo, m)


def _rms(x, gain):
    return x * lax.rsqrt(jnp.mean(x * x, axis=-1, keepdims=True) + EPS) * gain


def _inproj_kernel(x_ref, gmix_ref, w_ref, qg_ref, kg_ref, seg_ref,
                   qkv_ref, rkv_ref, small_ref, gate_ref):
    n = _rms(x_ref[...], gmix_ref[...]).astype(BF16)
    seg = seg_ref[...]

    def head_rms(t, gain):
        ss = _split_dot(t * t, seg)
        return t * lax.rsqrt(ss * (1.0 / HEAD_DIM) + EPS) * gain

    q = head_rms(_dot(n, w_ref[:, 0:WIDTH]), qg_ref[...]) * (HEAD_DIM ** -0.5)
    qkv_ref[:, 0:WIDTH] = q.astype(BF16)
    k = head_rms(_dot(n, w_ref[:, WIDTH:2 * WIDTH]), kg_ref[...])
    qkv_ref[:, WIDTH:2 * WIDTH] = k.astype(BF16)
    qkv_ref[:, 2 * WIDTH:3 * WIDTH] = _dot(n, w_ref[:, 2 * WIDTH:3 * WIDTH]).astype(BF16)
    c0 = 3 * WIDTH
    rkv_ref[...] = _dot(n, w_ref[:, c0:c0 + 3 * WIDTH])
    c0 += 3 * WIDTH
    small_ref[...] = _dot(n, w_ref[:, c0:c0 + SMALL_COLS])
    c0 += SMALL_COLS
    gate_ref[...] = jax.nn.sigmoid(_dot(n, w_ref[:, c0:c0 + 2 * D_MODEL]))


def _inproj(x2, gmix, w_in_p, qg, kg, seg, tm):
    m = x2.shape[0]
    const = lambda i: (0, 0)
    row = lambda i: (i, 0)
    return pl.pallas_call(
        _inproj_kernel,
        grid=(m // tm,),
        in_specs=[
            pl.BlockSpec((tm, D_MODEL), row),
            pl.BlockSpec((1, D_MODEL), const),
            pl.BlockSpec((D_MODEL, IN_COLS_PADDED), const, pipeline_mode=pl.Buffered(1)),
            pl.BlockSpec((1, WIDTH), const),
            pl.BlockSpec((1, WIDTH), const),
            pl.BlockSpec((WIDTH, WIDTH), const),
        ],
        out_specs=[
            pl.BlockSpec((tm, 3 * WIDTH), row),
            pl.BlockSpec((tm, 3 * WIDTH), row),
            pl.BlockSpec((tm, SMALL_COLS), row),
            pl.BlockSpec((tm, 2 * D_MODEL), row),
        ],
        out_shape=[
            jax.ShapeDtypeStruct((m, 3 * WIDTH), BF16),
            jax.ShapeDtypeStruct((m, 3 * WIDTH), F32),
            jax.ShapeDtypeStruct((m, SMALL_COLS), F32),
            jax.ShapeDtypeStruct((m, 2 * D_MODEL), F32),
        ],
        compiler_params=pltpu.CompilerParams(
            dimension_semantics=("arbitrary",), vmem_limit_bytes=VMEM_LIMIT),
        name="inproj",
    )(x2, gmix, w_in_p, qg, kg, seg)


def _natten_kernel(q_ref, k_ref, v_ref, bias_ref, o_ref, *, rows):
    r0 = pl.program_id(2) * NA_ROWS_PER_STEP
    win = WIN_R * GRID_W

    def body(i, carry):
        r = r0 + i
        row_start = jnp.clip(r - WIN_R // 2, 0, rows - WIN_R)
        off = r - row_start
        kstart = pl.multiple_of(row_start * GRID_W, GRID_W)
        qstart = pl.multiple_of(i * GRID_W, GRID_W)
        kw = k_ref[0, pl.ds(kstart, win), :]
        vw = v_ref[0, pl.ds(kstart, win), :]
        qrow = q_ref[0, pl.ds(qstart, GRID_W), :]
        outs = []
        for hh in range(2):
            sl = slice(hh * HEAD_DIM, (hh + 1) * HEAD_DIM)
            s = _dot_nt(qrow[:, sl], kw[:, sl]) + bias_ref[0, hh, off]
            e = jnp.exp(s - jnp.max(s, axis=-1, keepdims=True))
            l = jnp.sum(e, axis=-1, keepdims=True)
            outs.append(_dot(e.astype(BF16), vw[:, sl]) / l)
        o_ref[0, pl.ds(qstart, GRID_W), :] = jnp.concatenate(outs, axis=1).astype(BF16)
        return carry

    lax.fori_loop(0, NA_ROWS_PER_STEP, body, 0)


def _natten(qkv3, bias_tab):
    b, l, _ = qkv3.shape
    rows = l // GRID_W
    assert rows >= WIN_R and rows % NA_ROWS_PER_STEP == 0
    tq = NA_ROWS_PER_STEP * GRID_W
    npair = HEADS // 2
    return pl.pallas_call(
        functools.partial(_natten_kernel, rows=rows),
        grid=(b, npair, rows // NA_ROWS_PER_STEP),
        in_specs=[
            pl.BlockSpec((1, tq, 128), lambda bi, hp, ri: (bi, ri, hp)),
            pl.BlockSpec((1, l, 128), lambda bi, hp, ri: (bi, 0, npair + hp)),
            pl.BlockSpec((1, l, 128), lambda bi, hp, ri: (bi, 0, 2 * npair + hp)),
            pl.BlockSpec((1, 2, WIN_R, GRID_W, WIN_R * GRID_W), lambda bi, hp, ri: (hp, 0, 0, 0, 0)),
        ],
        out_specs=pl.BlockSpec((1, tq, 128), lambda bi, hp, ri: (bi, ri, hp)),
        out_shape=jax.ShapeDtypeStruct((b, l, WIDTH), BF16),
        compiler_params=pltpu.CompilerParams(
            dimension_semantics=("arbitrary", "arbitrary", "arbitrary"), vmem_limit_bytes=VMEM_LIMIT),
        name="natten",
    )(qkv3, qkv3, qkv3, bias_tab)


def _na_bias_table(rel_bias):
    qc = np.arange(GRID_W)[:, None]
    kc = np.arange(GRID_W)[None, :]
    dc = np.clip(kc - qc + WIN_C - 1, 0, 2 * WIN_C - 2)
    col_start = np.clip(qc - WIN_C // 2, 0, GRID_W - WIN_C)
    in_win = (kc >= col_start) & (kc < col_start + WIN_C)
    onehot = jnp.asarray(dc[:, :, None] == np.arange(2 * WIN_C - 1), F32)
    per_dr = jnp.einsum("hrc,qkc->hrqk", rel_bias.astype(F32), onehot, precision=lax.Precision.HIGHEST)
    per_dr = jnp.where(in_win[None, None], per_dr, NEG_INF)
    tab = jnp.stack([per_dr[:, WIN_R - 1 - off:2 * WIN_R - 1 - off].transpose(0, 2, 1, 3)
                     for off in range(WIN_R)], axis=1)
    return tab.reshape(HEADS // 2, 2, WIN_R, GRID_W, WIN_R * GRID_W)


def _rwprep_kernel(rkv_ref, rkv_prev_ref, rkv_next_ref, sm_ref, sm_prev_ref, sm_next_ref,
                   cw_rkv_ref, cw_sm_ref, wup_ref, w0_ref, aup_ref, a0_ref, gup_ref,
                   kk_ref, ka_ref, seg_ref,
                   r_out, kt_out, v_out, al_out, be_out, lwf_out, lwb_out, g_out, *, tm):
    i = pl.program_id(1)
    last = pl.num_programs(1) - 1

    def shifted(z, prev8, next8, cw):
        prow = jnp.where(i > 0, prev8[7:8, :], 0.0)
        nrow = jnp.where(i < last, next8[0:1, :], 0.0)
        rowid = lax.broadcasted_iota(jnp.int32, z.shape, 0)
        zp = jnp.where(rowid == 0, prow, pltpu.roll(z, 1, 0))
        zn = jnp.where(rowid == tm - 1, nrow, pltpu.roll(z, tm - 1, 0))
        return zp * cw[0:1, :] + z * cw[1:2, :] + zn * cw[2:3, :]

    u = shifted(rkv_ref[0], rkv_prev_ref[0], rkv_next_ref[0], cw_rkv_ref[...])
    us = shifted(sm_ref[0], sm_prev_ref[0], sm_next_ref[0], cw_sm_ref[...])
    r = u[:, 0:WIDTH]
    k = u[:, WIDTH:2 * WIDTH]
    v = u[:, 2 * WIDTH:3 * WIDTH]
    pre = _dot(jnp.tanh(us[:, 0:128]).astype(BF16), wup_ref[...]) + w0_ref[...]
    logw = -DECAY_SCALE * jax.nn.sigmoid(pre)
    a = jax.nn.sigmoid(_dot(us[:, 128:256].astype(BF16), aup_ref[...]) + a0_ref[...])
    g = _dot(jax.nn.sigmoid(us[:, 256:384]).astype(BF16), gup_ref[...])
    kk = k * kk_ref[...]
    ss = _split_dot(kk * kk, seg_ref[...])
    kk = kk / jnp.maximum(jnp.sqrt(ss), 1e-12)
    r_out[0] = r
    kt_out[0] = k * (1.0 + (a - 1.0) * ka_ref[...])
    v_out[0] = v
    al_out[0] = -kk
    be_out[0] = kk * a
    lwf_out[0] = logw[:, 0:WIDTH]
    lwb_out[0] = logw[:, WIDTH:2 * WIDTH]
    g_out[0] = g


def _rwprep(rkv3, sm3, wts, tm):
    b, l, _ = rkv3.shape
    nb8 = l // 8
    step8 = tm // 8
    cur = lambda bi, i: (bi, i, 0)
    prev = lambda bi, i: (bi, jnp.maximum(i * step8 - 1, 0), 0)
    nxt = lambda bi, i: (bi, jnp.minimum((i + 1) * step8, nb8 - 1), 0)
    const = lambda bi, i: (0, 0)
    wide = lambda c: pl.BlockSpec((1, tm, c), cur)
    out = jax.ShapeDtypeStruct((b, l, WIDTH), F32)
    return pl.pallas_call(
        functools.partial(_rwprep_kernel, tm=tm),
        grid=(b, l // tm),
        in_specs=[
            wide(3 * WIDTH), pl.BlockSpec((1, 8, 3 * WIDTH), prev), pl.BlockSpec((1, 8, 3 * WIDTH), nxt),
            wide(SMALL_COLS), pl.BlockSpec((1, 8, SMALL_COLS), prev), pl.BlockSpec((1, 8, SMALL_COLS), nxt),
            pl.BlockSpec((3, 3 * WIDTH), const), pl.BlockSpec((3, SMALL_COLS), const),
            pl.BlockSpec((128, 2 * WIDTH), const), pl.BlockSpec((1, 2 * WIDTH), const),
            pl.BlockSpec((128, WIDTH), const), pl.BlockSpec((1, WIDTH), const),
            pl.BlockSpec((128, WIDTH), const),
            pl.BlockSpec((1, WIDTH), const), pl.BlockSpec((1, WIDTH), const),
            pl.BlockSpec((WIDTH, WIDTH), const),
        ],
        out_specs=[wide(WIDTH)] * 8,
        out_shape=[out] * 8,
        compiler_params=pltpu.CompilerParams(
            dimension_semantics=("arbitrary", "arbitrary"), vmem_limit_bytes=VMEM_LIMIT),
        name="rwprep",
    )(rkv3, rkv3, rkv3, sm3, sm3, sm3, *wts)


def _scan_prepare(refs, reverse):
    r_ref, k_ref, v_ref, al_ref, be_ref, lw_ref = refs
    c = CHUNK
    ii = lax.broadcasted_iota(jnp.int32, (c, c), 0)
    jj = lax.broadcasted_iota(jnp.int32, (c, c), 1)
    tri = ((ii <= jj) if reverse else (ii >= jj)).astype(BF16)
    i2 = lax.broadcasted_iota(jnp.int32, (2 * c, 2 * c), 0)
    j2 = lax.broadcasted_iota(jnp.int32, (2 * c, 2 * c), 1) & (c - 1)
    lag = (j2 - (i2 & (c - 1))) if reverse else ((i2 & (c - 1)) - j2)
    mask = lag >= jnp.where(i2 < c, 1, 0)
    lw = lw_ref[0]
    lw_hi = lw.astype(BF16)
    lw_lo = (lw - lw_hi.astype(F32)).astype(BF16)
    g = _dot(tri, lw_hi) + _dot(tri, lw_lo)
    gtot = g[0:1, :] if reverse else g[c - 1:c, :]
    eng = jnp.exp(-g)
    egt = jnp.exp(gtot - g)
    be = be_ref[0]
    kt = k_ref[0]
    a_dec = al_ref[0] * jnp.exp(g - lw)
    r_dec = r_ref[0] * jnp.exp(g)
    return dict(
        mask=mask, a_dec=a_dec, v=v_ref[0], dec_tot=jnp.exp(gtot),
        x=jnp.concatenate([a_dec, r_dec], axis=0).astype(BF16),
        y=jnp.concatenate([be * eng, kt * eng], axis=0).astype(BF16),
        yh=jnp.concatenate([be * egt, kt * egt], axis=0).astype(BF16),
    )


def _rwscan_kernel(rf, kf, vf, af, bf, lwf, rb, kb, vb, ab, bb, lwb, yf_ref, yb_ref, s_ref):
    @pl.when(pl.program_id(1) == 0)
    def _():
        s_ref[...] = jnp.zeros_like(s_ref)

    c = CHUNK
    prep = (_scan_prepare((rf, kf, vf, af, bf, lwf), False),
            _scan_prepare((rb, kb, vb, ab, bb, lwb), True))
    chains = [(d, h) for d in range(2) for h in range(HEADS)]
    ids = range(len(chains))
    lanes = lambda h: slice(h * HEAD_DIM, (h + 1) * HEAD_DIM)

    sc = [jnp.where(prep[d]["mask"], _dot_nt(prep[d]["x"][:, lanes(h)], prep[d]["y"][:, lanes(h)]), 0.0)
          for d, h in chains]
    vh = [prep[d]["v"][:, lanes(h)] for d, h in chains]
    akv = [_dot(sc[i][0:c, c:2 * c].astype(BF16), vh[i].astype(BF16)) for i in ids]
    z = [jnp.concatenate([prep[d]["a_dec"][:, lanes(h)], akv[i]], axis=1) for i, (d, h) in enumerate(chains)]
    p = [sc[i][0:c, 0:c] for i in ids]
    for step in range(6):
        if step < 5:
            w = [_dot(p[i].astype(BF16), jnp.concatenate([z[i], p[i]], axis=1).astype(BF16)) for i in ids]
            z = [z[i] + w[i][:, 0:2 * HEAD_DIM] for i in ids]
            p = [w[i][:, 2 * HEAD_DIM:] for i in ids]
        else:
            z = [z[i] + _dot(p[i].astype(BF16), z[i].astype(BF16)) for i in ids]
    s_old = [s_ref[d, h] for d, h in chains]
    sb = [s.astype(BF16) for s in s_old]
    u = [_dot_nt(z[i][:, 0:HEAD_DIM].astype(BF16), sb[i]) + z[i][:, HEAD_DIM:] for i in ids]
    uv = [jnp.concatenate([u[i], vh[i]], axis=0).astype(BF16) for i in ids]
    out = [_dot_nt(prep[d]["x"][c:2 * c, lanes(h)], sb[i]) + _dot(sc[i][c:2 * c, :].astype(BF16), uv[i])
           for i, (d, h) in enumerate(chains)]
    for i, (d, h) in enumerate(chains):
        s_ref[d, h] = s_old[i] * prep[d]["dec_tot"][:, lanes(h)] + _dot_tn(uv[i], prep[d]["yh"][:, lanes(h)])
    yf_ref[0] = jnp.concatenate(out[0:HEADS], axis=1)
    yb_ref[0] = jnp.concatenate(out[HEADS:2 * HEADS], axis=1)


def _rwscan(r, kt, v, al, be, lwf, lwb):
    b, l, _ = r.shape
    nc = l // CHUNK
    fwd = lambda bi, n: (bi, n, 0)
    bwd = lambda bi, n: (bi, nc - 1 - n, 0)
    spec = lambda m: pl.BlockSpec((1, CHUNK, WIDTH), m)
    out = jax.ShapeDtypeStruct((b, l, WIDTH), F32)
    return pl.pallas_call(
        _rwscan_kernel,
        grid=(b, nc),
        in_specs=[spec(fwd)] * 6 + [spec(bwd)] * 6,
        out_specs=[spec(fwd), spec(bwd)],
        out_shape=[out, out],
        scratch_shapes=[pltpu.VMEM((2, HEADS, HEAD_DIM, HEAD_DIM), F32)],
        compiler_params=pltpu.CompilerParams(
            dimension_semantics=("arbitrary", "arbitrary"), vmem_limit_bytes=VMEM_LIMIT),
        name="rwscan",
    )(r, kt, v, al, be, lwf, r, kt, v, al, be, lwb)


def _merge_kernel(x_ref, ya_ref, yf_ref, yb_ref, r_ref, kt_ref, v_ref, g_ref, gate_ref,
                  lnw_ref, lnb_ref, rk_ref, seg_ref, wa_ref, wb_ref, wo_ref, h_ref):
    seg = seg_ref[...]
    o = yf_ref[...] + yb_ref[...]
    mu = _split_dot(o, seg) * (1.0 / HEAD_DIM)
    dlt = o - mu
    var = _split_dot(dlt * dlt, seg) * (1.0 / HEAD_DIM)
    o = dlt * lax.rsqrt(var + GN_EPS) * lnw_ref[...] + lnb_ref[...]
    v = v_ref[...]
    bonus = _split_dot(r_ref[...] * kt_ref[...] * rk_ref[...], seg) * v
    y_rw = ((o + bonus) * g_ref[...]).astype(BF16)
    merged = (gate_ref[:, 0:D_MODEL] * _dot(ya_ref[...], wa_ref[...])
              + gate_ref[:, D_MODEL:2 * D_MODEL] * _dot(y_rw, wb_ref[...]))
    h_ref[...] = x_ref[...] + _dot(merged.astype(BF16), wo_ref[...])


def _merge(x2, ya2, yf2, yb2, r2, kt2, v2, g2, gate2, lnw, lnb, rk, seg, wa, wb, wo, tm):
    m = x2.shape[0]
    const = lambda i: (0, 0)
    row = lambda i: (i, 0)
    rw = pl.BlockSpec((tm, WIDTH), row)
    vec = pl.BlockSpec((1, WIDTH), const)
    return pl.pallas_call(
        _merge_kernel,
        grid=(m // tm,),
        in_specs=[
            pl.BlockSpec((tm, D_MODEL), row), rw, rw, rw, rw, rw, rw, rw,
            pl.BlockSpec((tm, 2 * D_MODEL), row),
            vec, vec, vec,
            pl.BlockSpec((WIDTH, WIDTH), const),
            pl.BlockSpec((WIDTH, D_MODEL), const),
            pl.BlockSpec((WIDTH, D_MODEL), const),
            pl.BlockSpec((D_MODEL, D_MODEL), const),
        ],
        out_specs=pl.BlockSpec((tm, D_MODEL), row),
        out_shape=jax.ShapeDtypeStruct((m, D_MODEL), F32),
        compiler_params=pltpu.CompilerParams(
            dimension_semantics=("arbitrary",), vmem_limit_bytes=VMEM_LIMIT),
        name="merge",
    )(x2, ya2, yf2, yb2, r2, kt2, v2, g2, gate2, lnw, lnb, rk, seg, wa, wb, wo)


def _ffn_kernel(h_ref, p_ref, gffn_ref, w1_ref, w2_ref, gple_ref, wpg_ref, wple_ref, o_ref, *, ff_chunk):
    h = h_ref[...]
    n = _rms(h, gffn_ref[...]).astype(BF16)
    acc = h
    for c0 in range(0, D_FF, ff_chunk):
        f = jnp.square(jnp.maximum(_dot(n, w1_ref[:, c0:c0 + ff_chunk]), 0.0))
        acc = acc + _dot(f.astype(BF16), w2_ref[c0:c0 + ff_chunk, :])
    n2 = _rms(acc, gple_ref[...]).astype(BF16)
    pg = jax.nn.sigmoid(_dot(n2, wpg_ref[...]))
    o_ref[...] = acc + pg * _dot(p_ref[...].astype(BF16), wple_ref[...])


def _ffn(h2, p2, gffn, w1, w2, gple, wpg, wple, tm):
    m = h2.shape[0]
    const = lambda i: (0, 0)
    row = lambda i: (i, 0)
    once = lambda shape: pl.BlockSpec(shape, const, pipeline_mode=pl.Buffered(1))
    return pl.pallas_call(
        functools.partial(_ffn_kernel, ff_chunk=512),
        grid=(m // tm,),
        in_specs=[
            pl.BlockSpec((tm, D_MODEL), row),
            pl.BlockSpec((tm, PLE_DIM), row),
            pl.BlockSpec((1, D_MODEL), const),
            once((D_MODEL, D_FF)), once((D_FF, D_MODEL)),
            pl.BlockSpec((1, D_MODEL), const),
            once((D_MODEL, D_MODEL)), once((PLE_DIM, D_MODEL)),
        ],
        out_specs=pl.BlockSpec((tm, D_MODEL), row),
        out_shape=jax.ShapeDtypeStruct((m, D_MODEL), F32),
        compiler_params=pltpu.CompilerParams(
            dimension_semantics=("arbitrary",), vmem_limit_bytes=VMEM_LIMIT),
        name="ffn",
    )(h2, p2, gffn, w1, w2, gple, wpg, wple)


def _prepare_weights(g_mix, w_in, conv_w, q_gain, k_gain, rel_bias, w0_f, w_up_f, w0_b, w_up_b,
                     a0, a_up, g_up, k_k, k_a, r_k, ln_x_w, ln_x_b, w_a_out, w_b_out, w_o,
                     g_ffn, w_ff1, w_ff2, w_ple, g_ple, w_pgate):
    na = 3 * WIDTH
    rw0 = na
    sm0 = rw0 + 3 * WIDTH
    gate0 = sm0 + 2 * DECAY_RANK + AAA_RANK + GATE_RANK

    def regroup_small(t):
        lead = t[..., 0:2 * DECAY_RANK + AAA_RANK]
        pad = jnp.zeros(t.shape[:-1] + (128 - AAA_RANK,), t.dtype)
        return jnp.concatenate([lead, pad, t[..., 2 * DECAY_RANK + AAA_RANK:]], axis=-1)

    w_in_p = jnp.concatenate(
        [w_in[:, 0:sm0], regroup_small(w_in[:, sm0:gate0]), w_in[:, gate0:]], axis=1).astype(BF16)
    conv_rw = conv_w[:, 0:3 * WIDTH]
    conv_sm = regroup_small(conv_w[:, 3 * WIDTH:])
    zeros_up = jnp.zeros((DECAY_RANK, WIDTH), F32)
    w_up = jnp.concatenate([jnp.concatenate([w_up_f, zeros_up], axis=1),
                            jnp.concatenate([zeros_up, w_up_b], axis=1)], axis=0).astype(BF16)
    w0 = jnp.concatenate([w0_f, w0_b])[None, :]
    a_up_p = jnp.concatenate([a_up, jnp.zeros((128 - AAA_RANK, WIDTH), F32)], axis=0).astype(BF16)
    head_id = np.arange(WIDTH) // HEAD_DIM
    seg = jnp.asarray(head_id[:, None] == head_id[None, :], BF16)
    row = lambda t: t.reshape(1, -1)
    return dict(
        g_mix=row(g_mix), w_in=w_in_p, qg=row(jnp.tile(q_gain, HEADS)), kg=row(jnp.tile(k_gain, HEADS)),
        seg=seg, bias=_na_bias_table(rel_bias),
        prep=(conv_rw, conv_sm, w_up, w0, a_up_p, row(a0), g_up.astype(BF16), row(k_k), row(k_a), seg),
        lnw=row(ln_x_w), lnb=row(ln_x_b), rk=row(r_k),
        wa=w_a_out.astype(BF16), wb=w_b_out.astype(BF16), wo=w_o.astype(BF16),
        g_ffn=row(g_ffn), w1=w_ff1.astype(BF16), w2=w_ff2.astype(BF16),
        g_ple=row(g_ple), wpg=w_pgate.astype(BF16), wple=w_ple.astype(BF16),
    )


def _layer(x, p, w):
    b, l, _ = x.shape
    m = b * l
    tm = 512
    x2 = x.reshape(m, D_MODEL)
    qkv, rkv, small, gate = _inproj(x2, w["g_mix"], w["w_in"], w["qg"], w["kg"], w["seg"], tm)
    y_a = _natten(qkv.reshape(b, l, 3 * WIDTH), w["bias"])
    r, kt, v, al, be, lwf, lwb, g = _rwprep(
        rkv.reshape(b, l, 3 * WIDTH), small.reshape(b, l, SMALL_COLS), w["prep"], 256)
    y_f, y_b = _rwscan(r, kt, v, al, be, lwf, lwb)
    flat = lambda t: t.reshape(m, WIDTH)
    h = _merge(x2, flat(y_a), flat(y_f), flat(y_b), flat(r), flat(kt), flat(v), flat(g), gate,
               w["lnw"], w["lnb"], w["rk"], w["seg"], w["wa"], w["wb"], w["wo"], tm)
    out = _ffn(h, p.reshape(m, PLE_DIM), w["g_ffn"], w["w1"], w["w2"], w["g_ple"], w["wpg"], w["wple"], tm)
    return out.reshape(b, l, D_MODEL)


def kernel(x_prompt, x_sample, p_prompt, p_sample, g_mix, w_in, conv_w, q_gain, k_gain, rel_bias,
           w0_f, w_up_f, w0_b, w_up_b, a0, a_up, g_up, k_k, k_a, r_k, ln_x_w, ln_x_b,
           w_a_out, w_b_out, w_o, g_ffn, w_ff1, w_ff2, w_ple, g_ple, w_pgate):
    params = (g_mix, w_in, conv_w, q_gain, k_gain, rel_bias, w0_f, w_up_f, w0_b, w_up_b, a0, a_up, g_up,
              k_k, k_a, r_k, ln_x_w, ln_x_b, w_a_out, w_b_out, w_o, g_ffn, w_ff1, w_ff2, w_ple, g_ple, w_pgate)
    depth = g_mix.shape[0]

    def run(h, p):
        for i in range(depth):
            h = _layer(h, p[i], _prepare_weights(*[t[i] for t in params]))
        return h

    return (run(x_prompt, p_prompt), run(x_sample, p_sample))
```

```python
import functools

import numpy as np
import jax
import jax.numpy as jnp
from jax import lax
from jax.experimental import pallas as pl
from jax.experimental.pallas import tpu as pltpu

F32 = jnp.float32
BF16 = jnp.bfloat16

D_MODEL = 1024
PLE_DIM = 256
GRID_W = 64
WIN_R = 8
WIN_C = 16
HEADS = 8
HEAD_DIM = 64
WIDTH = HEADS * HEAD_DIM
DECAY_RANK = 64
AAA_RANK = 64
GATE_RANK = 128
D_FF = 4 * D_MODEL
EPS = 1e-6
GN_EPS = 64e-5
DECAY_SCALE = 0.606531
NEG_INF = -1e30

SMALL_COLS = 384
IN_COLS_PADDED = 3 * WIDTH + 3 * WIDTH + SMALL_COLS + 2 * D_MODEL
CHUNK = 64
NA_ROWS_PER_STEP = 8
NA_ROW_GROUP = 8
VMEM_LIMIT = 56 * 1024 * 1024


def _dot(a, b):
    return jnp.dot(a, b, preferred_element_type=F32)


def _dot_nt(a, b):
    return lax.dot_general(a, b, (((1,), (1,)), ((), ())), preferred_element_type=F32)


def _dot_tn(a, b):
    return lax.dot_general(a, b, (((0,), (0,)), ((), ())), preferred_element_type=F32)


def _split_dot(x, m):
    hi = x.astype(BF16)
    lo = (x - hi.astype(F32)).astype(BF16)
    return _dot(hi, m) + _dot(lo, m)


def _rms(x, gain):
    return x * lax.rsqrt(jnp.mean(x * x, axis=-1, keepdims=True) + EPS) * gain


def _inproj_kernel(x_ref, gmix_ref, w_ref, qg_ref, kg_ref, seg_ref,
                   qkv_ref, rkv_ref, small_ref, gate_ref):
    n = _rms(x_ref[...], gmix_ref[...]).astype(BF16)
    seg = seg_ref[...]

    def head_rms(t, gain):
        ss = _split_dot(t * t, seg)
        return t * lax.rsqrt(ss * (1.0 / HEAD_DIM) + EPS) * gain

    q = head_rms(_dot(n, w_ref[:, 0:WIDTH]), qg_ref[...]) * (HEAD_DIM ** -0.5)
    qkv_ref[:, 0:WIDTH] = q.astype(BF16)
    k = head_rms(_dot(n, w_ref[:, WIDTH:2 * WIDTH]), kg_ref[...])
    qkv_ref[:, WIDTH:2 * WIDTH] = k.astype(BF16)
    qkv_ref[:, 2 * WIDTH:3 * WIDTH] = _dot(n, w_ref[:, 2 * WIDTH:3 * WIDTH]).astype(BF16)
    c0 = 3 * WIDTH
    rkv_ref[...] = _dot(n, w_ref[:, c0:c0 + 3 * WIDTH])
    c0 += 3 * WIDTH
    small_ref[...] = _dot(n, w_ref[:, c0:c0 + SMALL_COLS])
    c0 += SMALL_COLS
    gate_ref[...] = jax.nn.sigmoid(_dot(n, w_ref[:, c0:c0 + 2 * D_MODEL])).astype(BF16)


def _inproj(x2, gmix, w_in_p, qg, kg, seg, tm):
    m = x2.shape[0]
    const = lambda i: (0, 0)
    row = lambda i: (i, 0)
    return pl.pallas_call(
        _inproj_kernel,
        grid=(m // tm,),
        in_specs=[
            pl.BlockSpec((tm, D_MODEL), row),
            pl.BlockSpec((1, D_MODEL), const),
            pl.BlockSpec((D_MODEL, IN_COLS_PADDED), const, pipeline_mode=pl.Buffered(1)),
            pl.BlockSpec((1, WIDTH), const),
            pl.BlockSpec((1, WIDTH), const),
            pl.BlockSpec((WIDTH, WIDTH), const),
        ],
        out_specs=[
            pl.BlockSpec((tm, 3 * WIDTH), row),
            pl.BlockSpec((tm, 3 * WIDTH), row),
            pl.BlockSpec((tm, SMALL_COLS), row),
            pl.BlockSpec((tm, 2 * D_MODEL), row),
        ],
        out_shape=[
            jax.ShapeDtypeStruct((m, 3 * WIDTH), BF16),
            jax.ShapeDtypeStruct((m, 3 * WIDTH), F32),
            jax.ShapeDtypeStruct((m, SMALL_COLS), F32),
            jax.ShapeDtypeStruct((m, 2 * D_MODEL), BF16),
        ],
        compiler_params=pltpu.CompilerParams(
            dimension_semantics=("arbitrary",), vmem_limit_bytes=VMEM_LIMIT),
        name="inproj",
    )(x2, gmix, w_in_p, qg, kg, seg)


def _natten_kernel(q_ref, k_ref, v_ref, bias_ref, o_ref, *, rows):
    r0 = pl.program_id(2) * NA_ROWS_PER_STEP
    win = WIN_R * GRID_W

    def body(gi, carry):
        qs, ks, vs, offs, hhs, qstarts = [], [], [], [], [], []
        for t in range(NA_ROW_GROUP):
            i = gi * NA_ROW_GROUP + t
            r = r0 + i
            row_start = jnp.clip(r - WIN_R // 2, 0, rows - WIN_R)
            kstart = pl.multiple_of(row_start * GRID_W, GRID_W)
            qstart = pl.multiple_of(i * GRID_W, GRID_W)
            kw = k_ref[0, pl.ds(kstart, win), :]
            vw = v_ref[0, pl.ds(kstart, win), :]
            qrow = q_ref[0, pl.ds(qstart, GRID_W), :]
            qstarts.append(qstart)
            for hh in range(2):
                sl = slice(hh * HEAD_DIM, (hh + 1) * HEAD_DIM)
                qs.append(qrow[:, sl])
                ks.append(kw[:, sl])
                vs.append(vw[:, sl])
                offs.append(r - row_start)
                hhs.append(hh)
        ids = range(len(qs))
        s = [_dot_nt(qs[c], ks[c]) + bias_ref[0, hhs[c], offs[c]] for c in ids]
        e = [jnp.exp(s[c] - jnp.max(s[c], axis=-1, keepdims=True)) for c in ids]
        l = [jnp.sum(e[c], axis=-1, keepdims=True) for c in ids]
        o = [_dot(e[c].astype(BF16), vs[c]) / l[c] for c in ids]
        for t in range(NA_ROW_GROUP):
            o_ref[0, pl.ds(qstarts[t], GRID_W), :] = jnp.concatenate(o[2 * t:2 * t + 2], axis=1).astype(BF16)
        return carry

    lax.fori_loop(0, NA_ROWS_PER_STEP // NA_ROW_GROUP, body, 0)


def _natten(qkv3, bias_tab):
    b, l, _ = qkv3.shape
    rows = l // GRID_W
    assert rows >= WIN_R and rows % NA_ROWS_PER_STEP == 0
    tq = NA_ROWS_PER_STEP * GRID_W
    npair = HEADS // 2
    return pl.pallas_call(
        functools.partial(_natten_kernel, rows=rows),
        grid=(b, npair, rows // NA_ROWS_PER_STEP),
        in_specs=[
            pl.BlockSpec((1, tq, 128), lambda bi, hp, ri: (bi, ri, hp)),
            pl.BlockSpec((1, l, 128), lambda bi, hp, ri: (bi, 0, npair + hp)),
            pl.BlockSpec((1, l, 128), lambda bi, hp, ri: (bi, 0, 2 * npair + hp)),
            pl.BlockSpec((1, 2, WIN_R, GRID_W, WIN_R * GRID_W), lambda bi, hp, ri: (hp, 0, 0, 0, 0)),
        ],
        out_specs=pl.BlockSpec((1, tq, 128), lambda bi, hp, ri: (bi, ri, hp)),
        out_shape=jax.ShapeDtypeStruct((b, l, WIDTH), BF16),
        compiler_params=pltpu.CompilerParams(
            dimension_semantics=("arbitrary", "arbitrary", "arbitrary"), vmem_limit_bytes=VMEM_LIMIT),
        name="natten",
    )(qkv3, qkv3, qkv3, bias_tab)


def _na_bias_table(rel_bias):
    qc = np.arange(GRID_W)[:, None]
    kc = np.arange(GRID_W)[None, :]
    dc = np.clip(kc - qc + WIN_C - 1, 0, 2 * WIN_C - 2)
    col_start = np.clip(qc - WIN_C // 2, 0, GRID_W - WIN_C)
    in_win = (kc >= col_start) & (kc < col_start + WIN_C)
    onehot = jnp.asarray(dc[:, :, None] == np.arange(2 * WIN_C - 1), F32)
    per_dr = jnp.einsum("hrc,qkc->hrqk", rel_bias.astype(F32), onehot, precision=lax.Precision.HIGHEST)
    per_dr = jnp.where(in_win[None, None], per_dr, NEG_INF)
    tab = jnp.stack([per_dr[:, WIN_R - 1 - off:2 * WIN_R - 1 - off].transpose(0, 2, 1, 3)
                     for off in range(WIN_R)], axis=1)
    return tab.reshape(HEADS // 2, 2, WIN_R, GRID_W, WIN_R * GRID_W)


def _rwprep_kernel(rkv_ref, rkv_prev_ref, rkv_next_ref, sm_ref, sm_prev_ref, sm_next_ref,
                   cw_rkv_ref, cw_sm_ref, wup_ref, w0_ref, aup_ref, a0_ref, gup_ref,
                   kk_ref, ka_ref, seg_ref,
                   r_out, kt_out, v_out, al_out, be_out, lwf_out, lwb_out, g_out, *, tm):
    i = pl.program_id(1)
    last = pl.num_programs(1) - 1

    def shifted(z, prev8, next8, cw):
        prow = jnp.where(i > 0, prev8[7:8, :], 0.0)
        nrow = jnp.where(i < last, next8[0:1, :], 0.0)
        rowid = lax.broadcasted_iota(jnp.int32, z.shape, 0)
        zp = jnp.where(rowid == 0, prow, pltpu.roll(z, 1, 0))
        zn = jnp.where(rowid == tm - 1, nrow, pltpu.roll(z, tm - 1, 0))
        return zp * cw[0:1, :] + z * cw[1:2, :] + zn * cw[2:3, :]

    u = shifted(rkv_ref[0], rkv_prev_ref[0], rkv_next_ref[0], cw_rkv_ref[...])
    us = shifted(sm_ref[0], sm_prev_ref[0], sm_next_ref[0], cw_sm_ref[...])
    r = u[:, 0:WIDTH]
    k = u[:, WIDTH:2 * WIDTH]
    v = u[:, 2 * WIDTH:3 * WIDTH]
    pre = _dot(jnp.tanh(us[:, 0:128]).astype(BF16), wup_ref[...]) + w0_ref[...]
    logw = -DECAY_SCALE * jax.nn.sigmoid(pre)
    a = jax.nn.sigmoid(_dot(us[:, 128:256].astype(BF16), aup_ref[...]) + a0_ref[...])
    g = _dot(jax.nn.sigmoid(us[:, 256:384]).astype(BF16), gup_ref[...])
    kk = k * kk_ref[...]
    ss = _split_dot(kk * kk, seg_ref[...])
    kk = kk / jnp.maximum(jnp.sqrt(ss), 1e-12)
    r_out[0] = r.astype(BF16)
    kt_out[0] = (k * (1.0 + (a - 1.0) * ka_ref[...])).astype(BF16)
    v_out[0] = v.astype(BF16)
    al_out[0] = (-kk).astype(BF16)
    be_out[0] = (kk * a).astype(BF16)
    lwf_out[0] = logw[:, 0:WIDTH]
    lwb_out[0] = logw[:, WIDTH:2 * WIDTH]
    g_out[0] = g.astype(BF16)


def _rwprep(rkv3, sm3, wts, tm):
    b, l, _ = rkv3.shape
    nb8 = l // 8
    step8 = tm // 8
    cur = lambda bi, i: (bi, i, 0)
    prev = lambda bi, i: (bi, jnp.maximum(i * step8 - 1, 0), 0)
    nxt = lambda bi, i: (bi, jnp.minimum((i + 1) * step8, nb8 - 1), 0)
    const = lambda bi, i: (0, 0)
    wide = lambda c: pl.BlockSpec((1, tm, c), cur)
    out = lambda dt: jax.ShapeDtypeStruct((b, l, WIDTH), dt)
    return pl.pallas_call(
        functools.partial(_rwprep_kernel, tm=tm),
        grid=(b, l // tm),
        in_specs=[
            wide(3 * WIDTH), pl.BlockSpec((1, 8, 3 * WIDTH), prev), pl.BlockSpec((1, 8, 3 * WIDTH), nxt),
            wide(SMALL_COLS), pl.BlockSpec((1, 8, SMALL_COLS), prev), pl.BlockSpec((1, 8, SMALL_COLS), nxt),
            pl.BlockSpec((3, 3 * WIDTH), const), pl.BlockSpec((3, SMALL_COLS), const),
            pl.BlockSpec((128, 2 * WIDTH), const), pl.BlockSpec((1, 2 * WIDTH), const),
            pl.BlockSpec((128, WIDTH), const), pl.BlockSpec((1, WIDTH), const),
            pl.BlockSpec((128, WIDTH), const),
            pl.BlockSpec((1, WIDTH), const), pl.BlockSpec((1, WIDTH), const),
            pl.BlockSpec((WIDTH, WIDTH), const),
        ],
        out_specs=[wide(WIDTH)] * 8,
        out_shape=[out(BF16)] * 5 + [out(F32)] * 2 + [out(BF16)],
        compiler_params=pltpu.CompilerParams(
            dimension_semantics=("arbitrary", "arbitrary"), vmem_limit_bytes=VMEM_LIMIT),
        name="rwprep",
    )(rkv3, rkv3, rkv3, sm3, sm3, sm3, *wts)


def _scan_prepare(refs, reverse):
    r_ref, k_ref, v_ref, al_ref, be_ref, lw_ref = refs
    c = CHUNK
    ii = lax.broadcasted_iota(jnp.int32, (c, c), 0)
    jj = lax.broadcasted_iota(jnp.int32, (c, c), 1)
    tri = ((ii <= jj) if reverse else (ii >= jj)).astype(BF16)
    i2 = lax.broadcasted_iota(jnp.int32, (2 * c, 2 * c), 0)
    j2 = lax.broadcasted_iota(jnp.int32, (2 * c, 2 * c), 1) & (c - 1)
    lag = (j2 - (i2 & (c - 1))) if reverse else ((i2 & (c - 1)) - j2)
    mask = lag >= jnp.where(i2 < c, 1, 0)
    lw = lw_ref[0]
    lw_hi = lw.astype(BF16)
    lw_lo = (lw - lw_hi.astype(F32)).astype(BF16)
    g = _dot(tri, lw_hi) + _dot(tri, lw_lo)
    gtot = g[0:1, :] if reverse else g[c - 1:c, :]
    eng = jnp.exp(-g)
    egt = jnp.exp(gtot - g)
    be = be_ref[0].astype(F32)
    kt = k_ref[0].astype(F32)
    a_dec = al_ref[0].astype(F32) * jnp.exp(g - lw)
    r_dec = r_ref[0].astype(F32) * jnp.exp(g)
    return dict(
        mask=mask, a_dec=a_dec, v=v_ref[0], dec_tot=jnp.exp(gtot),
        x=jnp.concatenate([a_dec, r_dec], axis=0).astype(BF16),
        y=jnp.concatenate([be * eng, kt * eng], axis=0).astype(BF16),
        yh=jnp.concatenate([be * egt, kt * egt], axis=0).astype(BF16),
    )


def _rwscan_kernel(rf, kf, vf, af, bf, lwf, rb, kb, vb, ab, bb, lwb, yf_ref, yb_ref, s_ref):
    @pl.when(pl.program_id(1) == 0)
    def _():
        s_ref[...] = jnp.zeros_like(s_ref)

    c = CHUNK
    prep = (_scan_prepare((rf, kf, vf, af, bf, lwf), False),
            _scan_prepare((rb, kb, vb, ab, bb, lwb), True))
    chains = [(d, h) for d in range(2) for h in range(HEADS)]
    ids = range(len(chains))
    lanes = lambda h: slice(h * HEAD_DIM, (h + 1) * HEAD_DIM)

    sc = [jnp.where(prep[d]["mask"], _dot_nt(prep[d]["x"][:, lanes(h)], prep[d]["y"][:, lanes(h)]), 0.0)
          for d, h in chains]
    vh = [prep[d]["v"][:, lanes(h)] for d, h in chains]
    akv = [_dot(sc[i][0:c, c:2 * c].astype(BF16), vh[i]) for i in ids]
    z = [jnp.concatenate([prep[d]["a_dec"][:, lanes(h)], akv[i]], axis=1) for i, (d, h) in enumerate(chains)]
    p = [sc[i][0:c, 0:c] for i in ids]
    for step in range(6):
        if step < 5:
            w = [_dot(p[i].astype(BF16), jnp.concatenate([z[i], p[i]], axis=1).astype(BF16)) for i in ids]
            z = [z[i] + w[i][:, 0:2 * HEAD_DIM] for i in ids]
            p = [w[i][:, 2 * HEAD_DIM:] for i in ids]
        else:
            z = [z[i] + _dot(p[i].astype(BF16), z[i].astype(BF16)) for i in ids]
    s_old = [s_ref[d, h] for d, h in chains]
    sb = [s.astype(BF16) for s in s_old]
    ar = [jnp.concatenate([z[i][:, 0:HEAD_DIM].astype(BF16), prep[d]["x"][c:2 * c, lanes(h)]], axis=0)
          for i, (d, h) in enumerate(chains)]
    ars = [_dot_nt(ar[i], sb[i]) for i in ids]
    u = [ars[i][0:c] + z[i][:, HEAD_DIM:] for i in ids]
    uv = [jnp.concatenate([u[i].astype(BF16), vh[i]], axis=0) for i in ids]
    out = [ars[i][c:2 * c] + _dot(sc[i][c:2 * c, :].astype(BF16), uv[i]) for i in ids]
    for i, (d, h) in enumerate(chains):
        s_ref[d, h] = s_old[i] * prep[d]["dec_tot"][:, lanes(h)] + _dot_tn(uv[i], prep[d]["yh"][:, lanes(h)])
    yf_ref[0] = jnp.concatenate(out[0:HEADS], axis=1)
    yb_ref[0] = jnp.concatenate(out[HEADS:2 * HEADS], axis=1)


def _rwscan(r, kt, v, al, be, lwf, lwb):
    b, l, _ = r.shape
    nc = l // CHUNK
    fwd = lambda bi, n: (bi, n, 0)
    bwd = lambda bi, n: (bi, nc - 1 - n, 0)
    spec = lambda m: pl.BlockSpec((1, CHUNK, WIDTH), m)
    out = jax.ShapeDtypeStruct((b, l, WIDTH), F32)
    return pl.pallas_call(
        _rwscan_kernel,
        grid=(b, nc),
        in_specs=[spec(fwd)] * 6 + [spec(bwd)] * 6,
        out_specs=[spec(fwd), spec(bwd)],
        out_shape=[out, out],
        scratch_shapes=[pltpu.VMEM((2, HEADS, HEAD_DIM, HEAD_DIM), F32)],
        compiler_params=pltpu.CompilerParams(
            dimension_semantics=("arbitrary", "arbitrary"), vmem_limit_bytes=VMEM_LIMIT),
        name="rwscan",
    )(r, kt, v, al, be, lwf, r, kt, v, al, be, lwb)


def _merge_kernel(x_ref, ya_ref, yf_ref, yb_ref, r_ref, kt_ref, v_ref, g_ref, gate_ref,
                  lnw_ref, lnb_ref, rk_ref, seg_ref, wa_ref, wb_ref, wo_ref, h_ref):
    seg = seg_ref[...]
    o = yf_ref[...] + yb_ref[...]
    mu = _split_dot(o, seg) * (1.0 / HEAD_DIM)
    dlt = o - mu
    var = _split_dot(dlt * dlt, seg) * (1.0 / HEAD_DIM)
    o = dlt * lax.rsqrt(var + GN_EPS) * lnw_ref[...] + lnb_ref[...]
    rk = r_ref[...].astype(F32) * kt_ref[...].astype(F32) * rk_ref[...]
    bonus = _split_dot(rk, seg) * v_ref[...].astype(F32)
    y_rw = ((o + bonus) * g_ref[...].astype(F32)).astype(BF16)
    merged = (gate_ref[:, 0:D_MODEL].astype(F32) * _dot(ya_ref[...], wa_ref[...])
              + gate_ref[:, D_MODEL:2 * D_MODEL].astype(F32) * _dot(y_rw, wb_ref[...]))
    h_ref[...] = x_ref[...] + _dot(merged.astype(BF16), wo_ref[...])


def _merge(x2, ya2, yf2, yb2, r2, kt2, v2, g2, gate2, lnw, lnb, rk, seg, wa, wb, wo, tm):
    m = x2.shape[0]
    const = lambda i: (0, 0)
    row = lambda i: (i, 0)
    rw = pl.BlockSpec((tm, WIDTH), row)
    vec = pl.BlockSpec((1, WIDTH), const)
    return pl.pallas_call(
        _merge_kernel,
        grid=(m // tm,),
        in_specs=[
            pl.BlockSpec((tm, D_MODEL), row), rw, rw, rw, rw, rw, rw, rw,
            pl.BlockSpec((tm, 2 * D_MODEL), row),
            vec, vec, vec,
            pl.BlockSpec((WIDTH, WIDTH), const),
            pl.BlockSpec((WIDTH, D_MODEL), const),
            pl.BlockSpec((WIDTH, D_MODEL), const),
            pl.BlockSpec((D_MODEL, D_MODEL), const),
        ],
        out_specs=pl.BlockSpec((tm, D_MODEL), row),
        out_shape=jax.ShapeDtypeStruct((m, D_MODEL), F32),
        compiler_params=pltpu.CompilerParams(
            dimension_semantics=("arbitrary",), vmem_limit_bytes=VMEM_LIMIT),
        name="merge",
    )(x2, ya2, yf2, yb2, r2, kt2, v2, g2, gate2, lnw, lnb, rk, seg, wa, wb, wo)


def _ffn_kernel(h_ref, p_ref, gffn_ref, w1_ref, w2_ref, gple_ref, wpg_ref, wple_ref, o_ref, *, ff_chunk):
    h = h_ref[...]
    n = _rms(h, gffn_ref[...]).astype(BF16)
    acc = h
    for c0 in range(0, D_FF, ff_chunk):
        f = jnp.square(jnp.maximum(_dot(n, w1_ref[:, c0:c0 + ff_chunk]), 0.0))
        acc = acc + _dot(f.astype(BF16), w2_ref[c0:c0 + ff_chunk, :])
    n2 = _rms(acc, gple_ref[...]).astype(BF16)
    pg = jax.nn.sigmoid(_dot(n2, wpg_ref[...]))
    o_ref[...] = acc + pg * _dot(p_ref[...].astype(BF16), wple_ref[...])


def _ffn(h2, p2, gffn, w1, w2, gple, wpg, wple, tm):
    m = h2.shape[0]
    const = lambda i: (0, 0)
    row = lambda i: (i, 0)
    once = lambda shape: pl.BlockSpec(shape, const, pipeline_mode=pl.Buffered(1))
    return pl.pallas_call(
        functools.partial(_ffn_kernel, ff_chunk=512),
        grid=(m // tm,),
        in_specs=[
            pl.BlockSpec((tm, D_MODEL), row),
            pl.BlockSpec((tm, PLE_DIM), row),
            pl.BlockSpec((1, D_MODEL), const),
            once((D_MODEL, D_FF)), once((D_FF, D_MODEL)),
            pl.BlockSpec((1, D_MODEL), const),
            once((D_MODEL, D_MODEL)), once((PLE_DIM, D_MODEL)),
        ],
        out_specs=pl.BlockSpec((tm, D_MODEL), row),
        out_shape=jax.ShapeDtypeStruct((m, D_MODEL), F32),
        compiler_params=pltpu.CompilerParams(
            dimension_semantics=("arbitrary",), vmem_limit_bytes=VMEM_LIMIT),
        name="ffn",
    )(h2, p2, gffn, w1, w2, gple, wpg, wple)


def _prepare_weights(g_mix, w_in, conv_w, q_gain, k_gain, rel_bias, w0_f, w_up_f, w0_b, w_up_b,
                     a0, a_up, g_up, k_k, k_a, r_k, ln_x_w, ln_x_b, w_a_out, w_b_out, w_o,
                     g_ffn, w_ff1, w_ff2, w_ple, g_ple, w_pgate):
    na = 3 * WIDTH
    rw0 = na
    sm0 = rw0 + 3 * WIDTH
    gate0 = sm0 + 2 * DECAY_RANK + AAA_RANK + GATE_RANK

    def regroup_small(t):
        lead = t[..., 0:2 * DECAY_RANK + AAA_RANK]
        pad = jnp.zeros(t.shape[:-1] + (128 - AAA_RANK,), t.dtype)
        return jnp.concatenate([lead, pad, t[..., 2 * DECAY_RANK + AAA_RANK:]], axis=-1)

    w_in_p = jnp.concatenate(
        [w_in[:, 0:sm0], regroup_small(w_in[:, sm0:gate0]), w_in[:, gate0:]], axis=1).astype(BF16)
    conv_rw = conv_w[:, 0:3 * WIDTH]
    conv_sm = regroup_small(conv_w[:, 3 * WIDTH:])
    zeros_up = jnp.zeros((DECAY_RANK, WIDTH), F32)
    w_up = jnp.concatenate([jnp.concatenate([w_up_f, zeros_up], axis=1),
                            jnp.concatenate([zeros_up, w_up_b], axis=1)], axis=0).astype(BF16)
    w0 = jnp.concatenate([w0_f, w0_b])[None, :]
    a_up_p = jnp.concatenate([a_up, jnp.zeros((128 - AAA_RANK, WIDTH), F32)], axis=0).astype(BF16)
    head_id = np.arange(WIDTH) // HEAD_DIM
    seg = jnp.asarray(head_id[:, None] == head_id[None, :], BF16)
    row = lambda t: t.reshape(1, -1)
    return dict(
        g_mix=row(g_mix), w_in=w_in_p, qg=row(jnp.tile(q_gain, HEADS)), kg=row(jnp.tile(k_gain, HEADS)),
        seg=seg, bias=_na_bias_table(rel_bias),
        prep=(conv_rw, conv_sm, w_up, w0, a_up_p, row(a0), g_up.astype(BF16), row(k_k), row(k_a), seg),
        lnw=row(ln_x_w), lnb=row(ln_x_b), rk=row(r_k),
        wa=w_a_out.astype(BF16), wb=w_b_out.astype(BF16), wo=w_o.astype(BF16),
        g_ffn=row(g_ffn), w1=w_ff1.astype(BF16), w2=w_ff2.astype(BF16),
        g_ple=row(g_ple), wpg=w_pgate.astype(BF16), wple=w_ple.astype(BF16),
    )


def _layer(x, p, w):
    b, l, _ = x.shape
    m = b * l
    tm = 512
    x2 = x.reshape(m, D_MODEL)
    qkv, rkv, small, gate = _inproj(x2, w["g_mix"], w["w_in"], w["qg"], w["kg"], w["seg"], tm)
    y_a = _natten(qkv.reshape(b, l, 3 * WIDTH), w["bias"])
    r, kt, v, al, be, lwf, lwb, g = _rwprep(
        rkv.reshape(b, l, 3 * WIDTH), small.reshape(b, l, SMALL_COLS), w["prep"], 256)
    y_f, y_b = _rwscan(r, kt, v, al, be, lwf, lwb)
    flat = lambda t: t.reshape(m, WIDTH)
    h = _merge(x2, flat(y_a), flat(y_f), flat(y_b), flat(r), flat(kt), flat(v), flat(g), gate,
               w["lnw"], w["lnb"], w["rk"], w["seg"], w["wa"], w["wb"], w["wo"], tm)
    out = _ffn(h, p.reshape(m, PLE_DIM), w["g_ffn"], w["w1"], w["w2"], w["g_ple"], w["wpg"], w["wple"], tm)
    return out.reshape(b, l, D_MODEL)


def kernel(x_prompt, x_sample, p_prompt, p_sample, g_mix, w_in, conv_w, q_gain, k_gain, rel_bias,
           w0_f, w_up_f, w0_b, w_up_b, a0, a_up, g_up, k_k, k_a, r_k, ln_x_w, ln_x_b,
           w_a_out, w_b_out, w_o, g_ffn, w_ff1, w_ff2, w_ple, g_ple, w_pgate):
    params = (g_mix, w_in, conv_w, q_gain, k_gain, rel_bias, w0_f, w_up_f, w0_b, w_up_b, a0, a_up, g_up,
              k_k, k_a, r_k, ln_x_w, ln_x_b, w_a_out, w_b_out, w_o, g_ffn, w_ff1, w_ff2, w_ple, g_ple, w_pgate)
    depth = g_mix.shape[0]

    def run(h, p):
        for i in range(depth):
            h = _layer(h, p[i], _prepare_weights(*[t[i] for t in params]))
        return h

    return (run(x_prompt, p_prompt), run(x_sample, p_sample))
```

```python
import functools

import numpy as np
import jax
import jax.numpy as jnp
from jax import lax
from jax.experimental import pallas as pl
from jax.experimental.pallas import tpu as pltpu

F32 = jnp.float32
BF16 = jnp.bfloat16

D_MODEL = 1024
PLE_DIM = 256
GRID_W = 64
WIN_R = 8
WIN_C = 16
HEADS = 8
HEAD_DIM = 64
WIDTH = HEADS * HEAD_DIM
DECAY_RANK = 64
AAA_RANK = 64
GATE_RANK = 128
D_FF = 4 * D_MODEL
EPS = 1e-6
GN_EPS = 64e-5
DECAY_SCALE = 0.606531
NEG_INF = -1e30

SMALL_COLS = 384
IN_COLS_PADDED = 3 * WIDTH + 3 * WIDTH + SMALL_COLS + 2 * D_MODEL
HALO = 16
CHUNK = 64
NA_ROWS_PER_STEP = 8
NA_ROW_GROUP = 8
VMEM_LIMIT = 56 * 1024 * 1024


def _dot(a, b):
    return jnp.dot(a, b, preferred_element_type=F32)


def _dot_nt(a, b):
    return lax.dot_general(a, b, (((1,), (1,)), ((), ())), preferred_element_type=F32)


def _dot_tn(a, b):
    return lax.dot_general(a, b, (((0,), (0,)), ((), ())), preferred_element_type=F32)


def _split_dot(x, m):
    hi = x.astype(BF16)
    lo = (x - hi.astype(F32)).astype(BF16)
    return _dot(hi, m) + _dot(lo, m)


def _rms(x, gain):
    return x * lax.rsqrt(jnp.mean(x * x, axis=-1, keepdims=True) + EPS) * gain


def _inproj_kernel(x_ref, xp_ref, xn_ref, gmix_ref, w_ref, qg_ref, kg_ref, seg_ref,
                   cw_rkv_ref, cw_sm_ref, wup_ref, w0_ref, aup_ref, a0_ref, gup_ref, kk_ref, ka_ref,
                   qkv_ref, gate_ref, r_out, kt_out, v_out, al_out, be_out, g_out, lwf_out, lwb_out,
                   *, tm, tiles_per_seq):
    pos = pl.program_id(0) % tiles_per_seq
    gmix = gmix_ref[...]
    n = _rms(x_ref[...], gmix).astype(BF16)
    n_prev = jnp.where(pos > 0, _rms(xp_ref[...], gmix), 0.0).astype(BF16)
    n_next = jnp.where(pos < tiles_per_seq - 1, _rms(xn_ref[...], gmix), 0.0).astype(BF16)
    n_ext = jnp.concatenate([n_prev, n, n_next], axis=0)
    seg = seg_ref[...]

    def head_rms(t, gain):
        ss = _split_dot(t * t, seg)
        return t * lax.rsqrt(ss * (1.0 / HEAD_DIM) + EPS) * gain

    q = head_rms(_dot(n, w_ref[:, 0:WIDTH]), qg_ref[...]) * (HEAD_DIM ** -0.5)
    qkv_ref[:, 0:WIDTH] = q.astype(BF16)
    k = head_rms(_dot(n, w_ref[:, WIDTH:2 * WIDTH]), kg_ref[...])
    qkv_ref[:, WIDTH:2 * WIDTH] = k.astype(BF16)
    qkv_ref[:, 2 * WIDTH:3 * WIDTH] = _dot(n, w_ref[:, 2 * WIDTH:3 * WIDTH]).astype(BF16)
    c_rw = 3 * WIDTH
    c_sm = c_rw + 3 * WIDTH
    c_gate = c_sm + SMALL_COLS
    gate_ref[...] = jax.nn.sigmoid(_dot(n, w_ref[:, c_gate:c_gate + 2 * D_MODEL])).astype(BF16)

    ext = tm + 2 * HALO

    def conv3(z, cw):
        zp = pltpu.roll(z, 1, 0)[HALO:HALO + tm]
        zn = pltpu.roll(z, ext - 1, 0)[HALO:HALO + tm]
        return zp * cw[0:1, :] + z[HALO:HALO + tm] * cw[1:2, :] + zn * cw[2:3, :]

    u = conv3(_dot(n_ext, w_ref[:, c_rw:c_rw + 3 * WIDTH]), cw_rkv_ref[...])
    us = conv3(_dot(n_ext, w_ref[:, c_sm:c_sm + SMALL_COLS]), cw_sm_ref[...])
    kx = u[:, WIDTH:2 * WIDTH]
    pre = _dot(jnp.tanh(us[:, 0:128]).astype(BF16), wup_ref[...]) + w0_ref[...]
    logw = -DECAY_SCALE * jax.nn.sigmoid(pre)
    a = jax.nn.sigmoid(_dot(us[:, 128:256].astype(BF16), aup_ref[...]) + a0_ref[...])
    kk = kx * kk_ref[...]
    kk = kk / jnp.maximum(jnp.sqrt(_split_dot(kk * kk, seg)), 1e-12)
    r_out[...] = u[:, 0:WIDTH].astype(BF16)
    kt_out[...] = (kx * (1.0 + (a - 1.0) * ka_ref[...])).astype(BF16)
    v_out[...] = u[:, 2 * WIDTH:3 * WIDTH].astype(BF16)
    al_out[...] = (-kk).astype(BF16)
    be_out[...] = (kk * a).astype(BF16)
    g_out[...] = _dot(jax.nn.sigmoid(us[:, 256:384]).astype(BF16), gup_ref[...]).astype(BF16)
    lwf_out[...] = logw[:, 0:WIDTH]
    lwb_out[...] = logw[:, WIDTH:2 * WIDTH]


def _inproj(x2, seq_len, gmix, w_in_p, qg, kg, seg, prep_w, tm):
    m = x2.shape[0]
    assert seq_len % tm == 0 and tm % HALO == 0
    nh = m // HALO
    per = tm // HALO
    const = lambda i: (0, 0)
    row = lambda i: (i, 0)
    prev = lambda i: (jnp.maximum(i * per - 1, 0), 0)
    nxt = lambda i: (jnp.minimum((i + 1) * per, nh - 1), 0)
    vec = lambda c: pl.BlockSpec((1, c), const)
    wide = pl.BlockSpec((tm, WIDTH), row)
    out = lambda c, dt: jax.ShapeDtypeStruct((m, c), dt)
    return pl.pallas_call(
        functools.partial(_inproj_kernel, tm=tm, tiles_per_seq=seq_len // tm),
        grid=(m // tm,),
        in_specs=[
            pl.BlockSpec((tm, D_MODEL), row),
            pl.BlockSpec((HALO, D_MODEL), prev),
            pl.BlockSpec((HALO, D_MODEL), nxt),
            vec(D_MODEL),
            pl.BlockSpec((D_MODEL, IN_COLS_PADDED), const, pipeline_mode=pl.Buffered(1)),
            vec(WIDTH), vec(WIDTH),
            pl.BlockSpec((WIDTH, WIDTH), const),
            pl.BlockSpec((3, 3 * WIDTH), const), pl.BlockSpec((3, SMALL_COLS), const),
            pl.BlockSpec((128, 2 * WIDTH), const), vec(2 * WIDTH),
            pl.BlockSpec((128, WIDTH), const), vec(WIDTH),
            pl.BlockSpec((128, WIDTH), const),
            vec(WIDTH), vec(WIDTH),
        ],
        out_specs=[pl.BlockSpec((tm, 3 * WIDTH), row), pl.BlockSpec((tm, 2 * D_MODEL), row)] + [wide] * 8,
        out_shape=[out(3 * WIDTH, BF16), out(2 * D_MODEL, BF16)] + [out(WIDTH, BF16)] * 6 + [out(WIDTH, F32)] * 2,
        compiler_params=pltpu.CompilerParams(
            dimension_semantics=("arbitrary",), vmem_limit_bytes=VMEM_LIMIT),
        name="inproj",
    )(x2, x2, x2, gmix, w_in_p, qg, kg, seg, *prep_w)


def _natten_kernel(q_ref, k_ref, v_ref, bias_ref, o_ref, *, rows):
    r0 = pl.program_id(2) * NA_ROWS_PER_STEP
    win = WIN_R * GRID_W

    def body(gi, carry):
        qs, ks, vs, offs, hhs, qstarts = [], [], [], [], [], []
        for t in range(NA_ROW_GROUP):
            i = gi * NA_ROW_GROUP + t
            r = r0 + i
            row_start = jnp.clip(r - WIN_R // 2, 0, rows - WIN_R)
            kstart = pl.multiple_of(row_start * GRID_W, GRID_W)
            qstart = pl.multiple_of(i * GRID_W, GRID_W)
            kw = k_ref[0, pl.ds(kstart, win), :]
            vw = v_ref[0, pl.ds(kstart, win), :]
            qrow = q_ref[0, pl.ds(qstart, GRID_W), :]
            qstarts.append(qstart)
            for hh in range(2):
                sl = slice(hh * HEAD_DIM, (hh + 1) * HEAD_DIM)
                qs.append(qrow[:, sl])
                ks.append(kw[:, sl])
                vs.append(vw[:, sl])
                offs.append(r - row_start)
                hhs.append(hh)
        ids = range(len(qs))
        s = [_dot_nt(qs[c], ks[c]) + bias_ref[0, hhs[c], offs[c]] for c in ids]
        e = [jnp.exp(s[c] - jnp.max(s[c], axis=-1, keepdims=True)) for c in ids]
        l = [jnp.sum(e[c], axis=-1, keepdims=True) for c in ids]
        o = [_dot(e[c].astype(BF16), vs[c]) / l[c] for c in ids]
        for t in range(NA_ROW_GROUP):
            o_ref[0, pl.ds(qstarts[t], GRID_W), :] = jnp.concatenate(o[2 * t:2 * t + 2], axis=1).astype(BF16)
        return carry

    lax.fori_loop(0, NA_ROWS_PER_STEP // NA_ROW_GROUP, body, 0)


def _natten(qkv3, bias_tab):
    b, l, _ = qkv3.shape
    rows = l // GRID_W
    assert rows >= WIN_R and rows % NA_ROWS_PER_STEP == 0
    tq = NA_ROWS_PER_STEP * GRID_W
    npair = HEADS // 2
    return pl.pallas_call(
        functools.partial(_natten_kernel, rows=rows),
        grid=(b, npair, rows // NA_ROWS_PER_STEP),
        in_specs=[
            pl.BlockSpec((1, tq, 128), lambda bi, hp, ri: (bi, ri, hp)),
            pl.BlockSpec((1, l, 128), lambda bi, hp, ri: (bi, 0, npair + hp)),
            pl.BlockSpec((1, l, 128), lambda bi, hp, ri: (bi, 0, 2 * npair + hp)),
            pl.BlockSpec((1, 2, WIN_R, GRID_W, WIN_R * GRID_W), lambda bi, hp, ri: (hp, 0, 0, 0, 0)),
        ],
        out_specs=pl.BlockSpec((1, tq, 128), lambda bi, hp, ri: (bi, ri, hp)),
        out_shape=jax.ShapeDtypeStruct((b, l, WIDTH), BF16),
        compiler_params=pltpu.CompilerParams(
            dimension_semantics=("arbitrary", "arbitrary", "arbitrary"), vmem_limit_bytes=VMEM_LIMIT),
        name="natten",
    )(qkv3, qkv3, qkv3, bias_tab)


def _na_bias_table(rel_bias):
    qc = np.arange(GRID_W)[:, None]
    kc = np.arange(GRID_W)[None, :]
    dc = np.clip(kc - qc + WIN_C - 1, 0, 2 * WIN_C - 2)
    col_start = np.clip(qc - WIN_C // 2, 0, GRID_W - WIN_C)
    in_win = (kc >= col_start) & (kc < col_start + WIN_C)
    onehot = jnp.asarray(dc[:, :, None] == np.arange(2 * WIN_C - 1), F32)
    per_dr = jnp.einsum("hrc,qkc->hrqk", rel_bias.astype(F32), onehot, precision=lax.Precision.HIGHEST)
    per_dr = jnp.where(in_win[None, None], per_dr, NEG_INF)
    tab = jnp.stack([per_dr[:, WIN_R - 1 - off:2 * WIN_R - 1 - off].transpose(0, 2, 1, 3)
                     for off in range(WIN_R)], axis=1)
    return tab.reshape(HEADS // 2, 2, WIN_R, GRID_W, WIN_R * GRID_W)


def _scan_prepare(refs, reverse):
    r_ref, k_ref, v_ref, al_ref, be_ref, lw_ref = refs
    c = CHUNK
    ii = lax.broadcasted_iota(jnp.int32, (c, c), 0)
    jj = lax.broadcasted_iota(jnp.int32, (c, c), 1)
    tri = ((ii <= jj) if reverse else (ii >= jj)).astype(BF16)
    lw = lw_ref[0]
    lw_hi = lw.astype(BF16)
    lw_lo = (lw - lw_hi.astype(F32)).astype(BF16)
    g = _dot(tri, lw_hi) + _dot(tri, lw_lo)
    gtot = g[0:1, :] if reverse else g[c - 1:c, :]
    eng = jnp.exp(-g)
    egt = jnp.exp(gtot - g)
    be = be_ref[0].astype(F32)
    kt = k_ref[0].astype(F32)
    return dict(
        a=al_ref[0].astype(F32) * jnp.exp(g - lw), r=r_ref[0].astype(F32) * jnp.exp(g),
        b=(be * eng).astype(BF16), k=(kt * eng).astype(BF16), bh=be * egt, kh=kt * egt,
        v=v_ref[0].astype(F32), dec_tot=jnp.exp(gtot))


def _rwscan_kernel(rf, kf, vf, af, bf, lwf, rb, kb, vb, ab, bb, lwb, mask_ref, yf_ref, yb_ref, s_ref):
    @pl.when(pl.program_id(1) == 0)
    def _():
        s_ref[...] = jnp.zeros_like(s_ref)

    c, hd = CHUNK, HEAD_DIM
    prep = (_scan_prepare((rf, kf, vf, af, bf, lwf), False),
            _scan_prepare((rb, kb, vb, ab, bb, lwb), True))
    keep = (mask_ref[0] != 0.0, mask_ref[1] != 0.0)
    chains = [(d, hp) for d in range(2) for hp in range(HEADS // 2)]
    ids = range(len(chains))
    pair = lambda t, hp: t[:, hp * 2 * hd:(hp + 1) * 2 * hd]
    low = lax.broadcasted_iota(jnp.int32, (c, 2 * hd), 1) < hd
    swap = lambda t: pltpu.roll(t, hd, 1)

    def block_rows(t):
        return jnp.concatenate([jnp.where(low, t, 0.0), jnp.where(low, 0.0, t)], axis=0)

    xa = [pair(prep[d]["a"], hp) for d, hp in chains]
    lhs = [jnp.concatenate([block_rows(xa[i]), block_rows(pair(prep[d]["r"], hp))], axis=0).astype(BF16)
           for i, (d, hp) in enumerate(chains)]
    rhs = [jnp.concatenate([pair(prep[d]["b"], hp)] * 2 + [pair(prep[d]["k"], hp)] * 2, axis=0)
           for d, hp in chains]
    sc = [jnp.where(keep[d], _dot_nt(lhs[i], rhs[i]), 0.0) for i, (d, hp) in enumerate(chains)]
    vc = [jnp.concatenate([jnp.where(low, t, 0.0), jnp.where(low, swap(t), 0.0)], axis=0)
          for t in (pair(prep[d]["v"], hp) for d, hp in chains)]
    vcb = [t.astype(BF16) for t in vc]
    z = [jnp.concatenate([jnp.where(low, 0.0, swap(xa[i])), jnp.where(low, 0.0, xa[i])], axis=0)
         + _dot(sc[i][0:2 * c, 2 * c:4 * c].astype(BF16), vcb[i]) for i in ids]
    p = [sc[i][0:2 * c, 0:2 * c] for i in ids]
    for step in range(6):
        if step < 5:
            w = [_dot(p[i].astype(BF16), jnp.concatenate([z[i], p[i]], axis=1).astype(BF16)) for i in ids]
            z = [z[i] + w[i][:, 0:2 * hd] for i in ids]
            p = [w[i][:, 2 * hd:] for i in ids]
        else:
            z = [z[i] + _dot(p[i].astype(BF16), z[i].astype(BF16)) for i in ids]
    s_old = [s_ref[d, hp] for d, hp in chains]
    at = [jnp.concatenate([jnp.where(low, swap(z[i][0:c]), 0.0), jnp.where(low, 0.0, z[i][c:2 * c])], axis=0)
          for i in ids]
    ars = [_dot_nt(jnp.concatenate([at[i].astype(BF16), lhs[i][2 * c:4 * c]], axis=0), s_old[i].astype(BF16))
           for i in ids]
    u = [ars[i][0:2 * c] + z[i][:, 0:hd] for i in ids]
    uv = [jnp.concatenate([u[i].astype(BF16), vcb[i][:, 0:hd]], axis=0) for i in ids]
    out = [ars[i][2 * c:4 * c] + _dot(sc[i][2 * c:4 * c, :].astype(BF16), uv[i]) for i in ids]
    for i, (d, hp) in enumerate(chains):
        yh = jnp.concatenate([block_rows(pair(prep[d]["bh"], hp)), block_rows(pair(prep[d]["kh"], hp))],
                             axis=0).astype(BF16)
        s_ref[d, hp] = s_old[i] * pair(prep[d]["dec_tot"], hp) + _dot_tn(uv[i], yh)
    y = [jnp.concatenate([out[i][0:c], out[i][c:2 * c]], axis=1) for i in ids]
    yf_ref[0] = jnp.concatenate(y[0:HEADS // 2], axis=1)
    yb_ref[0] = jnp.concatenate(y[HEADS // 2:HEADS], axis=1)


def _scan_masks():
    idx = np.arange(4 * CHUNK)
    head = (idx % (2 * CHUNK)) // CHUNK
    tok = idx % CHUNK
    a_row = idx < 2 * CHUNK
    lag = tok[:, None] - tok[None, :]
    same = head[:, None] == head[None, :]
    need = np.where(a_row, 1, 0)[:, None]
    return jnp.asarray(np.stack([same & (lag >= need), same & (-lag >= need)]), F32)


def _rwscan(r, kt, v, al, be, lwf, lwb):
    b, l, _ = r.shape
    nc = l // CHUNK
    fwd = lambda bi, n: (bi, n, 0)
    bwd = lambda bi, n: (bi, nc - 1 - n, 0)
    spec = lambda m: pl.BlockSpec((1, CHUNK, WIDTH), m)
    out = jax.ShapeDtypeStruct((b, l, WIDTH), F32)
    return pl.pallas_call(
        _rwscan_kernel,
        grid=(b, nc),
        in_specs=[spec(fwd)] * 6 + [spec(bwd)] * 6
        + [pl.BlockSpec((2, 4 * CHUNK, 4 * CHUNK), lambda bi, n: (0, 0, 0))],
        out_specs=[spec(fwd), spec(bwd)],
        out_shape=[out, out],
        scratch_shapes=[pltpu.VMEM((2, HEADS // 2, HEAD_DIM, 2 * HEAD_DIM), F32)],
        compiler_params=pltpu.CompilerParams(
            dimension_semantics=("arbitrary", "arbitrary"), vmem_limit_bytes=VMEM_LIMIT),
        name="rwscan",
    )(r, kt, v, al, be, lwf, r, kt, v, al, be, lwb, _scan_masks())


def _merge_kernel(x_ref, ya_ref, yf_ref, yb_ref, r_ref, kt_ref, v_ref, g_ref, gate_ref,
                  lnw_ref, lnb_ref, rk_ref, seg_ref, wa_ref, wb_ref, wo_ref, h_ref):
    seg = seg_ref[...]
    o = yf_ref[...] + yb_ref[...]
    mu = _split_dot(o, seg) * (1.0 / HEAD_DIM)
    dlt = o - mu
    var = _split_dot(dlt * dlt, seg) * (1.0 / HEAD_DIM)
    o = dlt * lax.rsqrt(var + GN_EPS) * lnw_ref[...] + lnb_ref[...]
    rk = r_ref[...].astype(F32) * kt_ref[...].astype(F32) * rk_ref[...]
    bonus = _split_dot(rk, seg) * v_ref[...].astype(F32)
    y_rw = ((o + bonus) * g_ref[...].astype(F32)).astype(BF16)
    merged = (gate_ref[:, 0:D_MODEL].astype(F32) * _dot(ya_ref[...], wa_ref[...])
              + gate_ref[:, D_MODEL:2 * D_MODEL].astype(F32) * _dot(y_rw, wb_ref[...]))
    h_ref[...] = x_ref[...] + _dot(merged.astype(BF16), wo_ref[...])


def _merge(x2, ya2, yf2, yb2, r2, kt2, v2, g2, gate2, lnw, lnb, rk, seg, wa, wb, wo, tm):
    m = x2.shape[0]
    const = lambda i: (0, 0)
    row = lambda i: (i, 0)
    rw = pl.BlockSpec((tm, WIDTH), row)
    vec = pl.BlockSpec((1, WIDTH), const)
    return pl.pallas_call(
        _merge_kernel,
        grid=(m // tm,),
        in_specs=[
            pl.BlockSpec((tm, D_MODEL), row), rw, rw, rw, rw, rw, rw, rw,
            pl.BlockSpec((tm, 2 * D_MODEL), row),
            vec, vec, vec,
            pl.BlockSpec((WIDTH, WIDTH), const),
            pl.BlockSpec((WIDTH, D_MODEL), const),
            pl.BlockSpec((WIDTH, D_MODEL), const),
            pl.BlockSpec((D_MODEL, D_MODEL), const),
        ],
        out_specs=pl.BlockSpec((tm, D_MODEL), row),
        out_shape=jax.ShapeDtypeStruct((m, D_MODEL), F32),
        compiler_params=pltpu.CompilerParams(
            dimension_semantics=("arbitrary",), vmem_limit_bytes=VMEM_LIMIT),
        name="merge",
    )(x2, ya2, yf2, yb2, r2, kt2, v2, g2, gate2, lnw, lnb, rk, seg, wa, wb, wo)


def _ffn_kernel(h_ref, p_ref, gffn_ref, w1_ref, w2_ref, gple_ref, wpg_ref, wple_ref, o_ref, *, ff_chunk):
    h = h_ref[...]
    n = _rms(h, gffn_ref[...]).astype(BF16)
    acc = h
    for c0 in range(0, D_FF, ff_chunk):
        f = jnp.square(jnp.maximum(_dot(n, w1_ref[:, c0:c0 + ff_chunk]), 0.0))
        acc = acc + _dot(f.astype(BF16), w2_ref[c0:c0 + ff_chunk, :])
    n2 = _rms(acc, gple_ref[...]).astype(BF16)
    pg = jax.nn.sigmoid(_dot(n2, wpg_ref[...]))
    o_ref[...] = acc + pg * _dot(p_ref[...].astype(BF16), wple_ref[...])


def _ffn(h2, p2, gffn, w1, w2, gple, wpg, wple, tm):
    m = h2.shape[0]
    const = lambda i: (0, 0)
    row = lambda i: (i, 0)
    once = lambda shape: pl.BlockSpec(shape, const, pipeline_mode=pl.Buffered(1))
    return pl.pallas_call(
        functools.partial(_ffn_kernel, ff_chunk=512),
        grid=(m // tm,),
        in_specs=[
            pl.BlockSpec((tm, D_MODEL), row),
            pl.BlockSpec((tm, PLE_DIM), row),
            pl.BlockSpec((1, D_MODEL), const),
            once((D_MODEL, D_FF)), once((D_FF, D_MODEL)),
            pl.BlockSpec((1, D_MODEL), const),
            once((D_MODEL, D_MODEL)), once((PLE_DIM, D_MODEL)),
        ],
        out_specs=pl.BlockSpec((tm, D_MODEL), row),
        out_shape=jax.ShapeDtypeStruct((m, D_MODEL), F32),
        compiler_params=pltpu.CompilerParams(
            dimension_semantics=("arbitrary",), vmem_limit_bytes=VMEM_LIMIT),
        name="ffn",
    )(h2, p2, gffn, w1, w2, gple, wpg, wple)


def _prepare_weights(g_mix, w_in, conv_w, q_gain, k_gain, rel_bias, w0_f, w_up_f, w0_b, w_up_b,
                     a0, a_up, g_up, k_k, k_a, r_k, ln_x_w, ln_x_b, w_a_out, w_b_out, w_o,
                     g_ffn, w_ff1, w_ff2, w_ple, g_ple, w_pgate):
    na = 3 * WIDTH
    rw0 = na
    sm0 = rw0 + 3 * WIDTH
    gate0 = sm0 + 2 * DECAY_RANK + AAA_RANK + GATE_RANK

    def regroup_small(t):
        lead = t[..., 0:2 * DECAY_RANK + AAA_RANK]
        pad = jnp.zeros(t.shape[:-1] + (128 - AAA_RANK,), t.dtype)
        return jnp.concatenate([lead, pad, t[..., 2 * DECAY_RANK + AAA_RANK:]], axis=-1)

    w_in_p = jnp.concatenate(
        [w_in[:, 0:sm0], regroup_small(w_in[:, sm0:gate0]), w_in[:, gate0:]], axis=1).astype(BF16)
    conv_rw = conv_w[:, 0:3 * WIDTH]
    conv_sm = regroup_small(conv_w[:, 3 * WIDTH:])
    zeros_up = jnp.zeros((DECAY_RANK, WIDTH), F32)
    w_up = jnp.concatenate([jnp.concatenate([w_up_f, zeros_up], axis=1),
                            jnp.concatenate([zeros_up, w_up_b], axis=1)], axis=0).astype(BF16)
    w0 = jnp.concatenate([w0_f, w0_b])[None, :]
    a_up_p = jnp.concatenate([a_up, jnp.zeros((128 - AAA_RANK, WIDTH), F32)], axis=0).astype(BF16)
    head_id = np.arange(WIDTH) // HEAD_DIM
    seg = jnp.asarray(head_id[:, None] == head_id[None, :], BF16)
    row = lambda t: t.reshape(1, -1)
    return dict(
        g_mix=row(g_mix), w_in=w_in_p, qg=row(jnp.tile(q_gain, HEADS)), kg=row(jnp.tile(k_gain, HEADS)),
        seg=seg, bias=_na_bias_table(rel_bias),
        prep=(conv_rw, conv_sm, w_up, w0, a_up_p, row(a0), g_up.astype(BF16), row(k_k), row(k_a)),
        lnw=row(ln_x_w), lnb=row(ln_x_b), rk=row(r_k),
        wa=w_a_out.astype(BF16), wb=w_b_out.astype(BF16), wo=w_o.astype(BF16),
        g_ffn=row(g_ffn), w1=w_ff1.astype(BF16), w2=w_ff2.astype(BF16),
        g_ple=row(g_ple), wpg=w_pgate.astype(BF16), wple=w_ple.astype(BF16),
    )


def _layer(x, p, w):
    b, l, _ = x.shape
    m = b * l
    tm = 512
    x2 = x.reshape(m, D_MODEL)
    qkv, gate, r, kt, v, al, be, g, lwf, lwb = _inproj(
        x2, l, w["g_mix"], w["w_in"], w["qg"], w["kg"], w["seg"], w["prep"], tm)
    y_a = _natten(qkv.reshape(b, l, 3 * WIDTH), w["bias"])
    seq = lambda t: t.reshape(b, l, WIDTH)
    y_f, y_b = _rwscan(seq(r), seq(kt), seq(v), seq(al), seq(be), seq(lwf), seq(lwb))
    flat = lambda t: t.reshape(m, WIDTH)
    h = _merge(x2, flat(y_a), flat(y_f), flat(y_b), r, kt, v, g, gate,
               w["lnw"], w["lnb"], w["rk"], w["seg"], w["wa"], w["wb"], w["wo"], tm)
    out = _ffn(h, p.reshape(m, PLE_DIM), w["g_ffn"], w["w1"], w["w2"], w["g_ple"], w["wpg"], w["wple"], tm)
    return out.reshape(b, l, D_MODEL)


def kernel(x_prompt, x_sample, p_prompt, p_sample, g_mix, w_in, conv_w, q_gain, k_gain, rel_bias,
           w0_f, w_up_f, w0_b, w_up_b, a0, a_up, g_up, k_k, k_a, r_k, ln_x_w, ln_x_b,
           w_a_out, w_b_out, w_o, g_ffn, w_ff1, w_ff2, w_ple, g_ple, w_pgate):
    params = (g_mix, w_in, conv_w, q_gain, k_gain, rel_bias, w0_f, w_up_f, w0_b, w_up_b, a0, a_up, g_up,
              k_k, k_a, r_k, ln_x_w, ln_x_b, w_a_out, w_b_out, w_o, g_ffn, w_ff1, w_ff2, w_ple, g_ple, w_pgate)
    depth = g_mix.shape[0]

    def run(h, p):
        for i in range(depth):
            h = _layer(h, p[i], _prepare_weights(*[t[i] for t in params]))
        return h

    return (run(x_prompt, p_prompt), run(x_sample, p_sample))
```

```python
import functools

import numpy as np
import jax
import jax.numpy as jnp
from jax import lax
from jax.experimental import pallas as pl
from jax.experimental.pallas import tpu as pltpu

F32 = jnp.float32
BF16 = jnp.bfloat16

D_MODEL = 1024
PLE_DIM = 256
GRID_W = 64
WIN_R = 8
WIN_C = 16
HEADS = 8
HEAD_DIM = 64
WIDTH = HEADS * HEAD_DIM
DECAY_RANK = 64
AAA_RANK = 64
GATE_RANK = 128
D_FF = 4 * D_MODEL
EPS = 1e-6
GN_EPS = 64e-5
DECAY_SCALE = 0.606531
NEG_INF = -1e30
LOG2E = 1.4426950408889634

SMALL_COLS = 384
IN_COLS_PADDED = 3 * WIDTH + 3 * WIDTH + SMALL_COLS + 2 * D_MODEL
HALO = 16
CHUNK = 64
SCAN_CHUNKS = 2
NA_ROWS_PER_STEP = 16
NA_ROW_GROUP = 8
VMEM_LIMIT = 56 * 1024 * 1024


def _dot(a, b):
    return jnp.dot(a, b, preferred_element_type=F32)


def _dot_nt(a, b):
    return lax.dot_general(a, b, (((1,), (1,)), ((), ())), preferred_element_type=F32)


def _dot_tn(a, b):
    return lax.dot_general(a, b, (((0,), (0,)), ((), ())), preferred_element_type=F32)


def _split_dot(x, m):
    hi = x.astype(BF16)
    lo = (x - hi.astype(F32)).astype(BF16)
    return _dot(hi, m) + _dot(lo, m)


def _rms(x, gain):
    return x * lax.rsqrt(jnp.mean(x * x, axis=-1, keepdims=True) + EPS) * gain


def _inproj_kernel(x_ref, xp_ref, xn_ref, gmix_ref, w_ref, qg_ref, kg_ref, seg_ref,
                   cw_rkv_ref, cw_sm_ref, wup_ref, w0_ref, aup_ref, a0_ref, gup_ref, kk_ref, ka_ref,
                   qkv_ref, gate_ref, r_out, kt_out, v_out, al_out, be_out, g_out, lwf_out, lwb_out,
                   *, tm, tiles_per_seq):
    pos = pl.program_id(0) % tiles_per_seq
    gmix = gmix_ref[...]
    n = _rms(x_ref[...], gmix).astype(BF16)
    n_prev = jnp.where(pos > 0, _rms(xp_ref[...], gmix), 0.0).astype(BF16)
    n_next = jnp.where(pos < tiles_per_seq - 1, _rms(xn_ref[...], gmix), 0.0).astype(BF16)
    n_ext = jnp.concatenate([n_prev, n, n_next], axis=0)
    seg = seg_ref[...]

    def head_rms(t, gain):
        ss = _split_dot(t * t, seg)
        return t * lax.rsqrt(ss * (1.0 / HEAD_DIM) + EPS) * gain

    q = head_rms(_dot(n, w_ref[:, 0:WIDTH]), qg_ref[...]) * (HEAD_DIM ** -0.5 * LOG2E)
    qkv_ref[:, 0:WIDTH] = q.astype(BF16)
    k = head_rms(_dot(n, w_ref[:, WIDTH:2 * WIDTH]), kg_ref[...])
    qkv_ref[:, WIDTH:2 * WIDTH] = k.astype(BF16)
    qkv_ref[:, 2 * WIDTH:3 * WIDTH] = _dot(n, w_ref[:, 2 * WIDTH:3 * WIDTH]).astype(BF16)
    c_rw = 3 * WIDTH
    c_sm = c_rw + 3 * WIDTH
    c_gate = c_sm + SMALL_COLS
    gate_ref[...] = jax.nn.sigmoid(_dot(n, w_ref[:, c_gate:c_gate + 2 * D_MODEL])).astype(BF16)

    ext = tm + 2 * HALO

    def conv3(z, cw):
        zp = pltpu.roll(z, 1, 0)[HALO:HALO + tm]
        zn = pltpu.roll(z, ext - 1, 0)[HALO:HALO + tm]
        return zp * cw[0:1, :] + z[HALO:HALO + tm] * cw[1:2, :] + zn * cw[2:3, :]

    u = conv3(_dot(n_ext, w_ref[:, c_rw:c_rw + 3 * WIDTH]), cw_rkv_ref[...])
    us = conv3(_dot(n_ext, w_ref[:, c_sm:c_sm + SMALL_COLS]), cw_sm_ref[...])
    kx = u[:, WIDTH:2 * WIDTH]
    pre = _dot(jnp.tanh(us[:, 0:128]).astype(BF16), wup_ref[...]) + w0_ref[...]
    logw = -DECAY_SCALE * jax.nn.sigmoid(pre)
    a = jax.nn.sigmoid(_dot(us[:, 128:256].astype(BF16), aup_ref[...]) + a0_ref[...])
    kk = kx * kk_ref[...]
    kk = kk / jnp.maximum(jnp.sqrt(_split_dot(kk * kk, seg)), 1e-12)
    r_out[...] = u[:, 0:WIDTH].astype(BF16)
    kt_out[...] = (kx * (1.0 + (a - 1.0) * ka_ref[...])).astype(BF16)
    v_out[...] = u[:, 2 * WIDTH:3 * WIDTH].astype(BF16)
    al_out[...] = (-kk).astype(BF16)
    be_out[...] = (kk * a).astype(BF16)
    g_out[...] = _dot(jax.nn.sigmoid(us[:, 256:384]).astype(BF16), gup_ref[...]).astype(BF16)
    lwf_out[...] = logw[:, 0:WIDTH]
    lwb_out[...] = logw[:, WIDTH:2 * WIDTH]


def _inproj(x2, seq_len, gmix, w_in_p, qg, kg, seg, prep_w, tm):
    m = x2.shape[0]
    assert seq_len % tm == 0 and tm % HALO == 0
    nh = m // HALO
    per = tm // HALO
    const = lambda i: (0, 0)
    row = lambda i: (i, 0)
    prev = lambda i: (jnp.maximum(i * per - 1, 0), 0)
    nxt = lambda i: (jnp.minimum((i + 1) * per, nh - 1), 0)
    vec = lambda c: pl.BlockSpec((1, c), const)
    wide = pl.BlockSpec((tm, WIDTH), row)
    out = lambda c, dt: jax.ShapeDtypeStruct((m, c), dt)
    return pl.pallas_call(
        functools.partial(_inproj_kernel, tm=tm, tiles_per_seq=seq_len // tm),
        grid=(m // tm,),
        in_specs=[
            pl.BlockSpec((tm, D_MODEL), row),
            pl.BlockSpec((HALO, D_MODEL), prev),
            pl.BlockSpec((HALO, D_MODEL), nxt),
            vec(D_MODEL),
            pl.BlockSpec((D_MODEL, IN_COLS_PADDED), const, pipeline_mode=pl.Buffered(1)),
            vec(WIDTH), vec(WIDTH),
            pl.BlockSpec((WIDTH, WIDTH), const),
            pl.BlockSpec((3, 3 * WIDTH), const), pl.BlockSpec((3, SMALL_COLS), const),
            pl.BlockSpec((128, 2 * WIDTH), const), vec(2 * WIDTH),
            pl.BlockSpec((128, WIDTH), const), vec(WIDTH),
            pl.BlockSpec((128, WIDTH), const),
            vec(WIDTH), vec(WIDTH),
        ],
        out_specs=[pl.BlockSpec((tm, 3 * WIDTH), row), pl.BlockSpec((tm, 2 * D_MODEL), row)] + [wide] * 8,
        out_shape=[out(3 * WIDTH, BF16), out(2 * D_MODEL, BF16)] + [out(WIDTH, BF16)] * 6 + [out(WIDTH, F32)] * 2,
        compiler_params=pltpu.CompilerParams(
            dimension_semantics=("arbitrary",), vmem_limit_bytes=VMEM_LIMIT),
        name="inproj",
    )(x2, x2, x2, gmix, w_in_p, qg, kg, seg, *prep_w)


def _natten_kernel(q_ref, k_ref, v_ref, bias_ref, o_ref, *, rows):
    r0 = pl.program_id(2) * NA_ROWS_PER_STEP
    win = WIN_R * GRID_W

    def body(gi, carry):
        low = lax.broadcasted_iota(jnp.int32, (GRID_W, 2 * HEAD_DIM), 1) < HEAD_DIM
        qs, ks, vs, offs, qstarts = [], [], [], [], []
        for t in range(NA_ROW_GROUP):
            i = gi * NA_ROW_GROUP + t
            r = r0 + i
            row_start = jnp.clip(r - WIN_R // 2, 0, rows - WIN_R)
            kstart = pl.multiple_of(row_start * GRID_W, GRID_W)
            qstart = pl.multiple_of(i * GRID_W, GRID_W)
            qrow = q_ref[0, pl.ds(qstart, GRID_W), :]
            qs.append(jnp.concatenate([jnp.where(low, qrow, 0), jnp.where(low, 0, qrow)], axis=0).astype(BF16))
            ks.append(k_ref[0, pl.ds(kstart, win), :])
            vs.append(v_ref[0, pl.ds(kstart, win), :])
            offs.append(r - row_start)
            qstarts.append(qstart)
        ids = range(NA_ROW_GROUP)
        s = [_dot_nt(qs[c], ks[c]) + bias_ref[0, offs[c]] for c in ids]
        e = [jnp.exp2(s[c] - jnp.max(s[c], axis=-1, keepdims=True)) for c in ids]
        l = [jnp.sum(e[c], axis=-1, keepdims=True) for c in ids]
        o = [_dot(e[c].astype(BF16), vs[c]) / l[c] for c in ids]
        for c in ids:
            o_ref[0, pl.ds(qstarts[c], GRID_W), :] = jnp.where(
                low, o[c][0:GRID_W], o[c][GRID_W:2 * GRID_W]).astype(BF16)
        return carry

    lax.fori_loop(0, NA_ROWS_PER_STEP // NA_ROW_GROUP, body, 0)


def _natten(qkv3, bias_tab):
    b, l, _ = qkv3.shape
    rows = l // GRID_W
    assert rows >= WIN_R and rows % NA_ROWS_PER_STEP == 0
    tq = NA_ROWS_PER_STEP * GRID_W
    npair = HEADS // 2
    return pl.pallas_call(
        functools.partial(_natten_kernel, rows=rows),
        grid=(b, npair, rows // NA_ROWS_PER_STEP),
        in_specs=[
            pl.BlockSpec((1, tq, 128), lambda bi, hp, ri: (bi, ri, hp)),
            pl.BlockSpec((1, l, 128), lambda bi, hp, ri: (bi, 0, npair + hp)),
            pl.BlockSpec((1, l, 128), lambda bi, hp, ri: (bi, 0, 2 * npair + hp)),
            pl.BlockSpec((1, WIN_R, 2 * GRID_W, WIN_R * GRID_W), lambda bi, hp, ri: (hp, 0, 0, 0)),
        ],
        out_specs=pl.BlockSpec((1, tq, 128), lambda bi, hp, ri: (bi, ri, hp)),
        out_shape=jax.ShapeDtypeStruct((b, l, WIDTH), BF16),
        compiler_params=pltpu.CompilerParams(
            dimension_semantics=("arbitrary", "arbitrary", "arbitrary"), vmem_limit_bytes=VMEM_LIMIT),
        name="natten",
    )(qkv3, qkv3, qkv3, bias_tab)


def _na_bias_table(rel_bias):
    qc = np.arange(GRID_W)[:, None]
    kc = np.arange(GRID_W)[None, :]
    dc = np.clip(kc - qc + WIN_C - 1, 0, 2 * WIN_C - 2)
    col_start = np.clip(qc - WIN_C // 2, 0, GRID_W - WIN_C)
    in_win = (kc >= col_start) & (kc < col_start + WIN_C)
    onehot = jnp.asarray(dc[:, :, None] == np.arange(2 * WIN_C - 1), F32)
    per_dr = jnp.einsum("hrc,qkc->hrqk", rel_bias.astype(F32), onehot, precision=lax.Precision.HIGHEST)
    per_dr = jnp.where(in_win[None, None], per_dr * LOG2E, NEG_INF)
    tab = jnp.stack([per_dr[:, WIN_R - 1 - off:2 * WIN_R - 1 - off].transpose(0, 2, 1, 3)
                     for off in range(WIN_R)], axis=1)
    tab = tab.reshape(HEADS // 2, 2, WIN_R, GRID_W, WIN_R * GRID_W).transpose(0, 2, 1, 3, 4)
    return tab.reshape(HEADS // 2, WIN_R, 2 * GRID_W, WIN_R * GRID_W)


def _scan_prepare(refs, rows, reverse):
    r_ref, k_ref, v_ref, al_ref, be_ref, lw_ref = refs
    c = CHUNK
    ii = lax.broadcasted_iota(jnp.int32, (c, c), 0)
    jj = lax.broadcasted_iota(jnp.int32, (c, c), 1)
    tri = ((ii <= jj) if reverse else (ii >= jj)).astype(BF16)
    lw = lw_ref[0, rows, :]
    lw_hi = lw.astype(BF16)
    lw_lo = (lw - lw_hi.astype(F32)).astype(BF16)
    g = _dot(tri, lw_hi) + _dot(tri, lw_lo)
    gtot = g[0:1, :] if reverse else g[c - 1:c, :]
    eng = jnp.exp(-g)
    egt = jnp.exp(gtot - g)
    be = be_ref[0, rows, :].astype(F32)
    kt = k_ref[0, rows, :].astype(F32)
    return dict(
        a=al_ref[0, rows, :].astype(F32) * jnp.exp(g - lw), r=r_ref[0, rows, :].astype(F32) * jnp.exp(g),
        b=(be * eng).astype(BF16), k=(kt * eng).astype(BF16), bh=be * egt, kh=kt * egt,
        v=v_ref[0, rows, :].astype(F32), dec_tot=jnp.exp(gtot))


def _rwscan_kernel(rf, kf, vf, af, bf, lwf, rb, kb, vb, ab, bb, lwb, mask_ref, yf_ref, yb_ref, s_ref):
    @pl.when(pl.program_id(1) == 0)
    def _():
        s_ref[...] = jnp.zeros_like(s_ref)

    c, hd, npair = CHUNK, HEAD_DIM, HEADS // 2
    refs = ((rf, kf, vf, af, bf, lwf), (rb, kb, vb, ab, bb, lwb))
    rows = lambda sub: slice(sub * c, (sub + 1) * c)
    prep = {(d, sub): _scan_prepare(refs[d], rows(sub), d == 1) for d in range(2) for sub in range(SCAN_CHUNKS)}
    keep = (mask_ref[0] != 0.0, mask_ref[1] != 0.0)
    chains = [(d, sub, hp) for d in range(2) for sub in range(SCAN_CHUNKS) for hp in range(npair)]
    ids = range(len(chains))
    pair = lambda t, hp: t[:, hp * 2 * hd:(hp + 1) * 2 * hd]
    low = lax.broadcasted_iota(jnp.int32, (c, 2 * hd), 1) < hd
    swap = lambda t: pltpu.roll(t, hd, 1)

    def block_rows(t):
        return jnp.concatenate([jnp.where(low, t, 0.0), jnp.where(low, 0.0, t)], axis=0)

    xa = [pair(prep[d, sub]["a"], hp) for d, sub, hp in chains]
    lhs = [jnp.concatenate([block_rows(xa[i]), block_rows(pair(prep[d, sub]["r"], hp))], axis=0).astype(BF16)
           for i, (d, sub, hp) in enumerate(chains)]
    rhs = [jnp.concatenate([pair(prep[d, sub]["b"], hp)] * 2 + [pair(prep[d, sub]["k"], hp)] * 2, axis=0)
           for d, sub, hp in chains]
    sc = [jnp.where(keep[d], _dot_nt(lhs[i], rhs[i]), 0.0) for i, (d, sub, hp) in enumerate(chains)]
    vcb = [jnp.concatenate([jnp.where(low, t, 0.0), jnp.where(low, swap(t), 0.0)], axis=0).astype(BF16)
           for t in (pair(prep[d, sub]["v"], hp) for d, sub, hp in chains)]
    z = [jnp.concatenate([jnp.where(low, 0.0, swap(xa[i])), jnp.where(low, 0.0, xa[i])], axis=0)
         + _dot(sc[i][0:2 * c, 2 * c:4 * c].astype(BF16), vcb[i]) for i in ids]
    p = [sc[i][0:2 * c, 0:2 * c] for i in ids]
    for step in range(6):
        if step < 5:
            w = [_dot(p[i].astype(BF16), jnp.concatenate([z[i], p[i]], axis=1).astype(BF16)) for i in ids]
            z = [z[i] + w[i][:, 0:2 * hd] for i in ids]
            p = [w[i][:, 2 * hd:] for i in ids]
        else:
            z = [z[i] + _dot(p[i].astype(BF16), z[i].astype(BF16)) for i in ids]
    ar = [jnp.concatenate([jnp.where(low, swap(z[i][0:c]), 0.0).astype(BF16),
                           jnp.where(low, 0.0, z[i][c:2 * c]).astype(BF16), lhs[i][2 * c:4 * c]], axis=0)
          for i in ids]
    yh = [jnp.concatenate([block_rows(pair(prep[d, sub]["bh"], hp)), block_rows(pair(prep[d, sub]["kh"], hp))],
                          axis=0).astype(BF16) for d, sub, hp in chains]

    state = {(d, hp): s_ref[d, hp] for d in range(2) for hp in range(npair)}
    y_refs = (yf_ref, yb_ref)
    for nth in range(SCAN_CHUNKS):
        cur = [i for i, (d, sub, hp) in enumerate(chains) if sub == (nth if d == 0 else SCAN_CHUNKS - 1 - nth)]
        ars = {i: _dot_nt(ar[i], state[chains[i][0], chains[i][2]].astype(BF16)) for i in cur}
        uv = {i: jnp.concatenate([(ars[i][0:2 * c] + z[i][:, 0:hd]).astype(BF16), vcb[i][:, 0:hd]], axis=0)
              for i in cur}
        out = {i: ars[i][2 * c:4 * c] + _dot(sc[i][2 * c:4 * c, :].astype(BF16), uv[i]) for i in cur}
        for i in cur:
            d, sub, hp = chains[i]
            state[d, hp] = state[d, hp] * pair(prep[d, sub]["dec_tot"], hp) + _dot_tn(uv[i], yh[i])
        for d in range(2):
            sub = nth if d == 0 else SCAN_CHUNKS - 1 - nth
            y = [jnp.concatenate([out[i][0:c], out[i][c:2 * c]], axis=1)
                 for i in cur if chains[i][0] == d]
            y_refs[d][0, rows(sub), :] = jnp.concatenate(y, axis=1)
    for (d, hp), s_new in state.items():
        s_ref[d, hp] = s_new


def _scan_masks():
    idx = np.arange(4 * CHUNK)
    head = (idx % (2 * CHUNK)) // CHUNK
    tok = idx % CHUNK
    a_row = idx < 2 * CHUNK
    lag = tok[:, None] - tok[None, :]
    same = head[:, None] == head[None, :]
    need = np.where(a_row, 1, 0)[:, None]
    return jnp.asarray(np.stack([same & (lag >= need), same & (-lag >= need)]), F32)


def _rwscan(r, kt, v, al, be, lwf, lwb):
    b, l, _ = r.shape
    step = SCAN_CHUNKS * CHUNK
    assert l % step == 0
    nc = l // step
    fwd = lambda bi, n: (bi, n, 0)
    bwd = lambda bi, n: (bi, nc - 1 - n, 0)
    spec = lambda m: pl.BlockSpec((1, step, WIDTH), m)
    out = jax.ShapeDtypeStruct((b, l, WIDTH), F32)
    return pl.pallas_call(
        _rwscan_kernel,
        grid=(b, nc),
        in_specs=[spec(fwd)] * 6 + [spec(bwd)] * 6
        + [pl.BlockSpec((2, 4 * CHUNK, 4 * CHUNK), lambda bi, n: (0, 0, 0))],
        out_specs=[spec(fwd), spec(bwd)],
        out_shape=[out, out],
        scratch_shapes=[pltpu.VMEM((2, HEADS // 2, HEAD_DIM, 2 * HEAD_DIM), F32)],
        compiler_params=pltpu.CompilerParams(
            dimension_semantics=("arbitrary", "arbitrary"), vmem_limit_bytes=VMEM_LIMIT),
        name="rwscan",
    )(r, kt, v, al, be, lwf, r, kt, v, al, be, lwb, _scan_masks())


def _tail_kernel(x_ref, ya_ref, yf_ref, yb_ref, r_ref, kt_ref, v_ref, g_ref, gate_ref, p_ref,
                 lnw_ref, lnb_ref, rk_ref, seg_ref, wa_ref, wb_ref, wo_ref,
                 gffn_ref, w1_ref, w2_ref, gple_ref, wpg_ref, wple_ref, o_ref, *, ff_chunk):
    seg = seg_ref[...]
    o = yf_ref[...] + yb_ref[...]
    mu = _split_dot(o, seg) * (1.0 / HEAD_DIM)
    dlt = o - mu
    var = _split_dot(dlt * dlt, seg) * (1.0 / HEAD_DIM)
    o = dlt * lax.rsqrt(var + GN_EPS) * lnw_ref[...] + lnb_ref[...]
    rk = r_ref[...].astype(F32) * kt_ref[...].astype(F32) * rk_ref[...]
    bonus = _split_dot(rk, seg) * v_ref[...].astype(F32)
    y_rw = ((o + bonus) * g_ref[...].astype(F32)).astype(BF16)
    merged = (gate_ref[:, 0:D_MODEL].astype(F32) * _dot(ya_ref[...], wa_ref[...])
              + gate_ref[:, D_MODEL:2 * D_MODEL].astype(F32) * _dot(y_rw, wb_ref[...]))
    h = x_ref[...] + _dot(merged.astype(BF16), wo_ref[...])
    n = _rms(h, gffn_ref[...]).astype(BF16)
    for c0 in range(0, D_FF, ff_chunk):
        f = jnp.square(jnp.maximum(_dot(n, w1_ref[:, c0:c0 + ff_chunk]), 0.0))
        h = h + _dot(f.astype(BF16), w2_ref[c0:c0 + ff_chunk, :])
    n2 = _rms(h, gple_ref[...]).astype(BF16)
    pg = jax.nn.sigmoid(_dot(n2, wpg_ref[...]))
    o_ref[...] = h + pg * _dot(p_ref[...].astype(BF16), wple_ref[...])


def _tail(x2, ya2, yf2, yb2, r2, kt2, v2, g2, gate2, p2, w, tm):
    m = x2.shape[0]
    const = lambda i: (0, 0)
    row = lambda i: (i, 0)
    rw = pl.BlockSpec((tm, WIDTH), row)
    vec = lambda c: pl.BlockSpec((1, c), const)
    once = lambda shape: pl.BlockSpec(shape, const, pipeline_mode=pl.Buffered(1))
    return pl.pallas_call(
        functools.partial(_tail_kernel, ff_chunk=512),
        grid=(m // tm,),
        in_specs=[
            pl.BlockSpec((tm, D_MODEL), row), rw, rw, rw, rw, rw, rw, rw,
            pl.BlockSpec((tm, 2 * D_MODEL), row),
            pl.BlockSpec((tm, PLE_DIM), row),
            vec(WIDTH), vec(WIDTH), vec(WIDTH),
            once((WIDTH, WIDTH)), once((WIDTH, D_MODEL)), once((WIDTH, D_MODEL)), once((D_MODEL, D_MODEL)),
            vec(D_MODEL), once((D_MODEL, D_FF)), once((D_FF, D_MODEL)),
            vec(D_MODEL), once((D_MODEL, D_MODEL)), once((PLE_DIM, D_MODEL)),
        ],
        out_specs=pl.BlockSpec((tm, D_MODEL), row),
        out_shape=jax.ShapeDtypeStruct((m, D_MODEL), F32),
        compiler_params=pltpu.CompilerParams(
            dimension_semantics=("arbitrary",), vmem_limit_bytes=VMEM_LIMIT),
        name="tail",
    )(x2, ya2, yf2, yb2, r2, kt2, v2, g2, gate2, p2,
      w["lnw"], w["lnb"], w["rk"], w["seg"], w["wa"], w["wb"], w["wo"],
      w["g_ffn"], w["w1"], w["w2"], w["g_ple"], w["wpg"], w["wple"])


def _prepare_weights(g_mix, w_in, conv_w, q_gain, k_gain, rel_bias, w0_f, w_up_f, w0_b, w_up_b,
                     a0, a_up, g_up, k_k, k_a, r_k, ln_x_w, ln_x_b, w_a_out, w_b_out, w_o,
                     g_ffn, w_ff1, w_ff2, w_ple, g_ple, w_pgate):
    na = 3 * WIDTH
    rw0 = na
    sm0 = rw0 + 3 * WIDTH
    gate0 = sm0 + 2 * DECAY_RANK + AAA_RANK + GATE_RANK

    def regroup_small(t):
        lead = t[..., 0:2 * DECAY_RANK + AAA_RANK]
        pad = jnp.zeros(t.shape[:-1] + (128 - AAA_RANK,), t.dtype)
        return jnp.concatenate([lead, pad, t[..., 2 * DECAY_RANK + AAA_RANK:]], axis=-1)

    w_in_p = jnp.concatenate(
        [w_in[:, 0:sm0], regroup_small(w_in[:, sm0:gate0]), w_in[:, gate0:]], axis=1).astype(BF16)
    conv_rw = conv_w[:, 0:3 * WIDTH]
    conv_sm = regroup_small(conv_w[:, 3 * WIDTH:])
    zeros_up = jnp.zeros((DECAY_RANK, WIDTH), F32)
    w_up = jnp.concatenate([jnp.concatenate([w_up_f, zeros_up], axis=1),
                            jnp.concatenate([zeros_up, w_up_b], axis=1)], axis=0).astype(BF16)
    w0 = jnp.concatenate([w0_f, w0_b])[None, :]
    a_up_p = jnp.concatenate([a_up, jnp.zeros((128 - AAA_RANK, WIDTH), F32)], axis=0).astype(BF16)
    head_id = np.arange(WIDTH) // HEAD_DIM
    seg = jnp.asarray(head_id[:, None] == head_id[None, :], BF16)
    row = lambda t: t.reshape(1, -1)
    return dict(
        g_mix=row(g_mix), w_in=w_in_p, qg=row(jnp.tile(q_gain, HEADS)), kg=row(jnp.tile(k_gain, HEADS)),
        seg=seg, bias=_na_bias_table(rel_bias),
        prep=(conv_rw, conv_sm, w_up, w0, a_up_p, row(a0), g_up.astype(BF16), row(k_k), row(k_a)),
        lnw=row(ln_x_w), lnb=row(ln_x_b), rk=row(r_k),
        wa=w_a_out.astype(BF16), wb=w_b_out.astype(BF16), wo=w_o.astype(BF16),
        g_ffn=row(g_ffn), w1=w_ff1.astype(BF16), w2=w_ff2.astype(BF16),
        g_ple=row(g_ple), wpg=w_pgate.astype(BF16), wple=w_ple.astype(BF16),
    )


def _layer(x, p, w):
    b, l, _ = x.shape
    m = b * l
    tm = 512
    x2 = x.reshape(m, D_MODEL)
    qkv, gate, r, kt, v, al, be, g, lwf, lwb = _inproj(
        x2, l, w["g_mix"], w["w_in"], w["qg"], w["kg"], w["seg"], w["prep"], tm)
    y_a = _natten(qkv.reshape(b, l, 3 * WIDTH), w["bias"])
    seq = lambda t: t.reshape(b, l, WIDTH)
    y_f, y_b = _rwscan(seq(r), seq(kt), seq(v), seq(al), seq(be), seq(lwf), seq(lwb))
    flat = lambda t: t.reshape(m, WIDTH)
    out = _tail(x2, flat(y_a), flat(y_f), flat(y_b), r, kt, v, g, gate, p.reshape(m, PLE_DIM), w, tm)
    return out.reshape(b, l, D_MODEL)


def kernel(x_prompt, x_sample, p_prompt, p_sample, g_mix, w_in, conv_w, q_gain, k_gain, rel_bias,
           w0_f, w_up_f, w0_b, w_up_b, a0, a_up, g_up, k_k, k_a, r_k, ln_x_w, ln_x_b,
           w_a_out, w_b_out, w_o, g_ffn, w_ff1, w_ff2, w_ple, g_ple, w_pgate):
    params = (g_mix, w_in, conv_w, q_gain, k_gain, rel_bias, w0_f, w_up_f, w0_b, w_up_b, a0, a_up, g_up,
              k_k, k_a, r_k, ln_x_w, ln_x_b, w_a_out, w_b_out, w_o, g_ffn, w_ff1, w_ff2, w_ple, g_ple, w_pgate)
    depth = g_mix.shape[0]

    def run(h, p):
        for i in range(depth):
            h = _layer(h, p[i], _prepare_weights(*[t[i] for t in params]))
        return h

    return (run(x_prompt, p_prompt), run(x_sample, p_sample))
```

```python
import functools

import numpy as np
import jax
import jax.numpy as jnp
from jax import lax
from jax.experimental import pallas as pl
from jax.experimental.pallas import tpu as pltpu

F32 = jnp.float32
BF16 = jnp.bfloat16

D_MODEL = 1024
PLE_DIM = 256
GRID_W = 64
WIN_R = 8
WIN_C = 16
HEADS = 8
HEAD_DIM = 64
WIDTH = HEADS * HEAD_DIM
DECAY_RANK = 64
AAA_RANK = 64
GATE_RANK = 128
D_FF = 4 * D_MODEL
EPS = 1e-6
GN_EPS = 64e-5
DECAY_SCALE = 0.606531
NEG_INF = -1e30
LOG2E = 1.4426950408889634

SMALL_COLS = 384
IN_COLS_PADDED = 3 * WIDTH + 3 * WIDTH + SMALL_COLS + 2 * D_MODEL
HALO = 16
CHUNK = 64
SCAN_CHUNKS = 4
NA_ROWS_PER_STEP = 16
NA_ROW_GROUP = 8
VMEM_LIMIT = 56 * 1024 * 1024


def _dot(a, b):
    return jnp.dot(a, b, preferred_element_type=F32)


def _dot_nt(a, b):
    return lax.dot_general(a, b, (((1,), (1,)), ((), ())), preferred_element_type=F32)


def _dot_tn(a, b):
    return lax.dot_general(a, b, (((0,), (0,)), ((), ())), preferred_element_type=F32)


def _seg_sum(x):
    low =lax.broadcasted_iota(jnp.int32, (x.shape[0], 2 * HEAD_DIM), 1) < HEAD_DIM
    outs = []
    for c0 in range(0, x.shape[1], 2 * HEAD_DIM):
        slab = x[:, c0:c0 + 2 * HEAD_DIM]
        lo = jnp.sum(jnp.where(low, slab, 0.0), axis=-1, keepdims=True)
        hi = jnp.sum(jnp.where(low, 0.0, slab), axis=-1, keepdims=True)
        outs.append(jnp.where(low, lo, hi))
    return jnp.concatenate(outs, axis=1)


def _rms(x, gain):
    return x * lax.rsqrt(jnp.mean(x * x, axis=-1, keepdims=True) + EPS) * gain


def _inproj_kernel(x_ref, xp_ref, xn_ref, gmix_ref, w_ref, qg_ref, kg_ref,
                   cw_rkv_ref, cw_sm_ref, wup_ref, w0_ref, aup_ref, a0_ref, gup_ref, kk_ref, ka_ref,
                   qkv_ref, gate_ref, r_out, kt_out, v_out, al_out, be_out, g_out, lwf_out, lwb_out,
                   *, tm, tiles_per_seq):
    pos = pl.program_id(0) % tiles_per_seq
    gmix = gmix_ref[...]
    n = _rms(x_ref[...], gmix).astype(BF16)
    n_prev = jnp.where(pos > 0, _rms(xp_ref[...], gmix), 0.0).astype(BF16)
    n_next = jnp.where(pos < tiles_per_seq - 1, _rms(xn_ref[...], gmix), 0.0).astype(BF16)
    n_ext = jnp.concatenate([n_prev, n, n_next], axis=0)

    def head_rms(t, gain):
        ss = _seg_sum(t * t)
        return t * lax.rsqrt(ss * (1.0 / HEAD_DIM) + EPS) * gain

    q = head_rms(_dot(n, w_ref[:, 0:WIDTH]), qg_ref[...]) * (HEAD_DIM ** -0.5 * LOG2E)
    qkv_ref[:, 0:WIDTH] = q.astype(BF16)
    k = head_rms(_dot(n, w_ref[:, WIDTH:2 * WIDTH]), kg_ref[...])
    qkv_ref[:, WIDTH:2 * WIDTH] = k.astype(BF16)
    qkv_ref[:, 2 * WIDTH:3 * WIDTH] = _dot(n, w_ref[:, 2 * WIDTH:3 * WIDTH]).astype(BF16)
    c_rw = 3 * WIDTH
    c_sm = c_rw + 3 * WIDTH
    c_gate = c_sm + SMALL_COLS
    gate_ref[...] = jax.nn.sigmoid(_dot(n, w_ref[:, c_gate:c_gate + 2 * D_MODEL])).astype(BF16)

    ext = tm + 2 * HALO

    def conv3(z, cw):
        zp = pltpu.roll(z, 1, 0)[HALO:HALO + tm]
        zn = pltpu.roll(z, ext - 1, 0)[HALO:HALO + tm]
        return zp * cw[0:1, :] + z[HALO:HALO + tm] * cw[1:2, :] + zn * cw[2:3, :]

    u = conv3(_dot(n_ext, w_ref[:, c_rw:c_rw + 3 * WIDTH]), cw_rkv_ref[...])
    us = conv3(_dot(n_ext, w_ref[:, c_sm:c_sm + SMALL_COLS]), cw_sm_ref[...])
    kx = u[:, WIDTH:2 * WIDTH]
    pre = _dot(jnp.tanh(us[:, 0:128]).astype(BF16), wup_ref[...]) + w0_ref[...]
    logw = -DECAY_SCALE * jax.nn.sigmoid(pre)
    a = jax.nn.sigmoid(_dot(us[:, 128:256].astype(BF16), aup_ref[...]) + a0_ref[...])
    kk = kx * kk_ref[...]
    kk = kk / jnp.maximum(jnp.sqrt(_seg_sum(kk * kk)), 1e-12)
    r_out[...] = u[:, 0:WIDTH].astype(BF16)
    kt_out[...] = (kx * (1.0 + (a - 1.0) * ka_ref[...])).astype(BF16)
    v_out[...] = u[:, 2 * WIDTH:3 * WIDTH].astype(BF16)
    al_out[...] = (-kk).astype(BF16)
    be_out[...] = (kk * a).astype(BF16)
    g_out[...] = _dot(jax.nn.sigmoid(us[:, 256:384]).astype(BF16), gup_ref[...]).astype(BF16)
    lwf_out[...] = logw[:, 0:WIDTH]
    lwb_out[...] = logw[:, WIDTH:2 * WIDTH]


def _inproj(x2, seq_len, gmix, w_in_p, qg, kg, prep_w, tm):
    m = x2.shape[0]
    assert seq_len % tm == 0 and tm % HALO == 0
    nh = m // HALO
    per = tm // HALO
    const = lambda i: (0, 0)
    row = lambda i: (i, 0)
    prev = lambda i: (jnp.maximum(i * per - 1, 0), 0)
    nxt = lambda i: (jnp.minimum((i + 1) * per, nh - 1), 0)
    vec = lambda c: pl.BlockSpec((1, c), const)
    wide = pl.BlockSpec((tm, WIDTH), row)
    out = lambda c, dt: jax.ShapeDtypeStruct((m, c), dt)
    return pl.pallas_call(
        functools.partial(_inproj_kernel, tm=tm, tiles_per_seq=seq_len // tm),
        grid=(m // tm,),
        in_specs=[
            pl.BlockSpec((tm, D_MODEL), row),
            pl.BlockSpec((HALO, D_MODEL), prev),
            pl.BlockSpec((HALO, D_MODEL), nxt),
            vec(D_MODEL),
            pl.BlockSpec((D_MODEL, IN_COLS_PADDED), const, pipeline_mode=pl.Buffered(1)),
            vec(WIDTH), vec(WIDTH),
            pl.BlockSpec((3, 3 * WIDTH), const), pl.BlockSpec((3, SMALL_COLS), const),
            pl.BlockSpec((128, 2 * WIDTH), const), vec(2 * WIDTH),
            pl.BlockSpec((128, WIDTH), const), vec(WIDTH),
            pl.BlockSpec((128, WIDTH), const),
            vec(WIDTH), vec(WIDTH),
        ],
        out_specs=[pl.BlockSpec((tm, 3 * WIDTH), row), pl.BlockSpec((tm, 2 * D_MODEL), row)] + [wide] * 8,
        out_shape=[out(3 * WIDTH, BF16), out(2 * D_MODEL, BF16)] + [out(WIDTH, BF16)] * 6 + [out(WIDTH, F32)] * 2,
        compiler_params=pltpu.CompilerParams(
            dimension_semantics=("arbitrary",), vmem_limit_bytes=VMEM_LIMIT),
        name="inproj",
    )(x2, x2, x2, gmix, w_in_p, qg, kg, *prep_w)


def _natten_kernel(q_ref, k_ref, v_ref, bias_ref, o_ref, *, rows):
    r0 = pl.program_id(2) * NA_ROWS_PER_STEP
    win = WIN_R * GRID_W

    def body(gi, carry):
        low = lax.broadcasted_iota(jnp.int32, (GRID_W, 2 * HEAD_DIM), 1) < HEAD_DIM
        qs, ks, vs, offs, qstarts = [], [], [], [], []
        for t in range(NA_ROW_GROUP):
            i = gi * NA_ROW_GROUP + t
            r = r0 + i
            row_start = jnp.clip(r - WIN_R // 2, 0, rows - WIN_R)
            kstart = pl.multiple_of(row_start * GRID_W, GRID_W)
            qstart = pl.multiple_of(i * GRID_W, GRID_W)
            qrow = q_ref[0, pl.ds(qstart, GRID_W), :]
            qs.append(jnp.concatenate([jnp.where(low, qrow, 0), jnp.where(low, 0, qrow)], axis=0).astype(BF16))
            ks.append(k_ref[0, pl.ds(kstart, win), :])
            vs.append(v_ref[0, pl.ds(kstart, win), :])
            offs.append(r - row_start)
            qstarts.append(qstart)
        ids = range(NA_ROW_GROUP)
        s = [_dot_nt(qs[c], ks[c]) + bias_ref[0, offs[c]] for c in ids]
        e = [jnp.exp2(s[c] - jnp.max(s[c], axis=-1, keepdims=True)) for c in ids]
        l = [jnp.sum(e[c], axis=-1, keepdims=True) for c in ids]
        o = [_dot(e[c].astype(BF16), vs[c]) / l[c] for c in ids]
        for c in ids:
            o_ref[0, pl.ds(qstarts[c], GRID_W), :] = jnp.where(
                low, o[c][0:GRID_W], o[c][GRID_W:2 * GRID_W]).astype(BF16)
        return carry

    lax.fori_loop(0, NA_ROWS_PER_STEP // NA_ROW_GROUP, body, 0)


def _natten(qkv3, bias_tab):
    b, l, _ = qkv3.shape
    rows = l // GRID_W
    assert rows >= WIN_R and rows % NA_ROWS_PER_STEP == 0
    tq = NA_ROWS_PER_STEP * GRID_W
    npair = HEADS // 2
    return pl.pallas_call(
        functools.partial(_natten_kernel, rows=rows),
        grid=(b, npair, rows // NA_ROWS_PER_STEP),
        in_specs=[
            pl.BlockSpec((1, tq, 128), lambda bi, hp, ri: (bi, ri, hp)),
            pl.BlockSpec((1, l, 128), lambda bi, hp, ri: (bi, 0, npair + hp)),
            pl.BlockSpec((1, l, 128), lambda bi, hp, ri: (bi, 0, 2 * npair + hp)),
            pl.BlockSpec((1, WIN_R, 2 * GRID_W, WIN_R * GRID_W), lambda bi, hp, ri: (hp, 0, 0, 0)),
        ],
        out_specs=pl.BlockSpec((1, tq, 128), lambda bi, hp, ri: (bi, ri, hp)),
        out_shape=jax.ShapeDtypeStruct((b, l, WIDTH), BF16),
        compiler_params=pltpu.CompilerParams(
            dimension_semantics=("arbitrary", "arbitrary", "arbitrary"), vmem_limit_bytes=VMEM_LIMIT),
        name="natten",
    )(qkv3, qkv3, qkv3, bias_tab)


def _na_bias_table(rel_bias):
    qc = np.arange(GRID_W)[:, None]
    kc = np.arange(GRID_W)[None, :]
    dc = np.clip(kc - qc + WIN_C - 1, 0, 2 * WIN_C - 2)
    col_start = np.clip(qc - WIN_C // 2, 0, GRID_W - WIN_C)
    in_win = (kc >= col_start) & (kc < col_start + WIN_C)
    onehot = jnp.asarray(dc[:, :, None] == np.arange(2 * WIN_C - 1), F32)
    per_dr = jnp.einsum("hrc,qkc->hrqk", rel_bias.astype(F32), onehot, precision=lax.Precision.HIGHEST)
    per_dr = jnp.where(in_win[None, None], per_dr * LOG2E, NEG_INF)
    tab = jnp.stack([per_dr[:, WIN_R - 1 - off:2 * WIN_R - 1 - off].transpose(0, 2, 1, 3)
                     for off in range(WIN_R)], axis=1)
    tab = tab.reshape(HEADS // 2, 2, WIN_R, GRID_W, WIN_R * GRID_W).transpose(0, 2, 1, 3, 4)
    return tab.reshape(HEADS // 2, WIN_R, 2 * GRID_W, WIN_R * GRID_W)


def _scan_prepare(refs, rows, reverse):
    r_ref, k_ref, v_ref, al_ref, be_ref, lw_ref = refs
    c = CHUNK
    ii = lax.broadcasted_iota(jnp.int32, (c, c), 0)
    jj = lax.broadcasted_iota(jnp.int32, (c, c), 1)
    tri = ((ii <= jj) if reverse else (ii >= jj)).astype(BF16)
    lw = lw_ref[0, rows, :]
    lw_hi = lw.astype(BF16)
    lw_lo = (lw - lw_hi.astype(F32)).astype(BF16)
    g = _dot(tri, lw_hi) + _dot(tri, lw_lo)
    gtot = g[0:1, :] if reverse else g[c - 1:c, :]
    eng = jnp.exp(-g)
    egt = jnp.exp(gtot - g)
    be = be_ref[0, rows, :].astype(F32)
    kt = k_ref[0, rows, :].astype(F32)
    return dict(
        a=al_ref[0, rows, :].astype(F32) * jnp.exp(g - lw), r=r_ref[0, rows, :].astype(F32) * jnp.exp(g),
        b=(be * eng).astype(BF16), k=(kt * eng).astype(BF16), bh=be * egt, kh=kt * egt,
        v=v_ref[0, rows, :].astype(F32), dec_tot=jnp.exp(gtot))


def _rwscan_kernel(rf, kf, vf, af, bf, lwf, rb, kb, vb, ab, bb, lwb, mask_ref, yf_ref, yb_ref, s_ref):
    @pl.when(pl.program_id(1) == 0)
    def _():
        s_ref[...] = jnp.zeros_like(s_ref)

    c, hd, npair = CHUNK, HEAD_DIM, HEADS // 2
    refs = ((rf, kf, vf, af, bf, lwf), (rb, kb, vb, ab, bb, lwb))
    rows = lambda sub: slice(sub * c, (sub + 1) * c)
    prep = {(d, sub): _scan_prepare(refs[d], rows(sub), d == 1) for d in range(2) for sub in range(SCAN_CHUNKS)}
    keep = (mask_ref[0] != 0.0, mask_ref[1] != 0.0)
    chains = [(d, sub, hp) for d in range(2) for sub in range(SCAN_CHUNKS) for hp in range(npair)]
    ids = range(len(chains))
    pair = lambda t, hp: t[:, hp * 2 * hd:(hp + 1) * 2 * hd]
    low = lax.broadcasted_iota(jnp.int32, (c, 2 * hd), 1) < hd
    swap = lambda t: pltpu.roll(t, hd, 1)

    def block_rows(t):
        return jnp.concatenate([jnp.where(low, t, 0.0), jnp.where(low, 0.0, t)], axis=0)

    xa = [pair(prep[d, sub]["a"], hp) for d, sub, hp in chains]
    lhs = [jnp.concatenate([block_rows(xa[i]), block_rows(pair(prep[d, sub]["r"], hp))], axis=0).astype(BF16)
           for i, (d, sub, hp) in enumerate(chains)]
    rhs = [jnp.concatenate([pair(prep[d, sub]["b"], hp)] * 2 + [pair(prep[d, sub]["k"], hp)] * 2, axis=0)
           for d, sub, hp in chains]
    sc = [jnp.where(keep[d], _dot_nt(lhs[i], rhs[i]), 0.0) for i, (d, sub, hp) in enumerate(chains)]
    vcb = [jnp.concatenate([jnp.where(low, t, 0.0), jnp.where(low, swap(t), 0.0)], axis=0).astype(BF16)
           for t in (pair(prep[d, sub]["v"], hp) for d, sub, hp in chains)]
    z = [jnp.concatenate([jnp.where(low, 0.0, swap(xa[i])), jnp.where(low, 0.0, xa[i])], axis=0)
         + _dot(sc[i][0:2 * c, 2 * c:4 * c].astype(BF16), vcb[i]) for i in ids]
    p = [sc[i][0:2 * c, 0:2 * c] for i in ids]
    for step in range(6):
        if step < 5:
            w = [_dot(p[i].astype(BF16), jnp.concatenate([z[i], p[i]], axis=1).astype(BF16)) for i in ids]
            z = [z[i] + w[i][:, 0:2 * hd] for i in ids]
            p = [w[i][:, 2 * hd:] for i in ids]
        else:
            z = [z[i] + _dot(p[i].astype(BF16), z[i].astype(BF16)) for i in ids]
    ar = [jnp.concatenate([jnp.where(low, swap(z[i][0:c]), 0.0).astype(BF16),
                           jnp.where(low, 0.0, z[i][c:2 * c]).astype(BF16), lhs[i][2 * c:4 * c]], axis=0)
          for i in ids]
    yh = [jnp.concatenate([block_rows(pair(prep[d, sub]["bh"], hp)), block_rows(pair(prep[d, sub]["kh"], hp))],
                          axis=0).astype(BF16) for d, sub, hp in chains]

    state = {(d, hp): s_ref[d, hp] for d in range(2) for hp in range(npair)}
    y_refs = (yf_ref, yb_ref)
    for nth in range(SCAN_CHUNKS):
        cur = [i for i, (d, sub, hp) in enumerate(chains) if sub == (nth if d == 0 else SCAN_CHUNKS - 1 - nth)]
        ars = {i: _dot_nt(ar[i], state[chains[i][0], chains[i][2]].astype(BF16)) for i in cur}
        uv = {i: jnp.concatenate([(ars[i][0:2 * c] + z[i][:, 0:hd]).astype(BF16), vcb[i][:, 0:hd]], axis=0)
              for i in cur}
        out = {i: ars[i][2 * c:4 * c] + _dot(sc[i][2 * c:4 * c, :].astype(BF16), uv[i]) for i in cur}
        for i in cur:
            d, sub, hp = chains[i]
            state[d, hp] = state[d, hp] * pair(prep[d, sub]["dec_tot"], hp) + _dot_tn(uv[i], yh[i])
        for d in range(2):
            sub = nth if d == 0 else SCAN_CHUNKS - 1 - nth
            y = [jnp.concatenate([out[i][0:c], out[i][c:2 * c]], axis=1)
                 for i in cur if chains[i][0] == d]
            y_refs[d][0, rows(sub), :] = jnp.concatenate(y, axis=1)
    for (d, hp), s_new in state.items():
        s_ref[d, hp] = s_new


def _scan_masks():
    idx = np.arange(4 * CHUNK)
    head = (idx % (2 * CHUNK)) // CHUNK
    tok = idx % CHUNK
    a_row = idx < 2 * CHUNK
    lag = tok[:, None] - tok[None, :]
    same = head[:, None] == head[None, :]
    need = np.where(a_row, 1, 0)[:, None]
    return jnp.asarray(np.stack([same & (lag >= need), same & (-lag >= need)]), F32)


def _rwscan(r, kt, v, al, be, lwf, lwb):
    b, l, _ = r.shape
    step = SCAN_CHUNKS * CHUNK
    assert l % step == 0
    nc = l // step
    fwd = lambda bi, n: (bi, n, 0)
    bwd = lambda bi, n: (bi, nc - 1 - n, 0)
    spec = lambda m: pl.BlockSpec((1, step, WIDTH), m)
    out = jax.ShapeDtypeStruct((b, l, WIDTH), F32)
    return pl.pallas_call(
        _rwscan_kernel,
        grid=(b, nc),
        in_specs=[spec(fwd)] * 6 + [spec(bwd)] * 6
        + [pl.BlockSpec((2, 4 * CHUNK, 4 * CHUNK), lambda bi, n: (0, 0, 0))],
        out_specs=[spec(fwd), spec(bwd)],
        out_shape=[out, out],
        scratch_shapes=[pltpu.VMEM((2, HEADS // 2, HEAD_DIM, 2 * HEAD_DIM), F32)],
        compiler_params=pltpu.CompilerParams(
            dimension_semantics=("arbitrary", "arbitrary"), vmem_limit_bytes=VMEM_LIMIT),
        name="rwscan",
    )(r, kt, v, al, be, lwf, r, kt, v, al, be, lwb, _scan_masks())


def _tail_kernel(x_ref, ya_ref, yf_ref, yb_ref, r_ref, kt_ref, v_ref, g_ref, gate_ref, p_ref,
                 lnw_ref, lnb_ref, rk_ref, wa_ref, wb_ref, wo_ref,
                 gffn_ref, w1_ref, w2_ref, gple_ref, wpg_ref, wple_ref, o_ref, *, ff_chunk):
    o = yf_ref[...] + yb_ref[...]
    mu = _seg_sum(o) * (1.0 / HEAD_DIM)
    dlt = o - mu
    var = _seg_sum(dlt * dlt) * (1.0 / HEAD_DIM)
    o = dlt * lax.rsqrt(var + GN_EPS) * lnw_ref[...] + lnb_ref[...]
    rk = r_ref[...].astype(F32) * kt_ref[...].astype(F32) * rk_ref[...]
    bonus = _seg_sum(rk) * v_ref[...].astype(F32)
    y_rw = ((o + bonus) * g_ref[...].astype(F32)).astype(BF16)
    merged = (gate_ref[:, 0:D_MODEL].astype(F32) * _dot(ya_ref[...], wa_ref[...])
              + gate_ref[:, D_MODEL:2 * D_MODEL].astype(F32) * _dot(y_rw, wb_ref[...]))
    h = x_ref[...] + _dot(merged.astype(BF16), wo_ref[...])
    n = _rms(h, gffn_ref[...]).astype(BF16)
    for c0 in range(0, D_FF, ff_chunk):
        f = jnp.square(jnp.maximum(_dot(n, w1_ref[:, c0:c0 + ff_chunk]), 0.0))
        h = h + _dot(f.astype(BF16), w2_ref[c0:c0 + ff_chunk, :])
    n2 = _rms(h, gple_ref[...]).astype(BF16)
    pg = jax.nn.sigmoid(_dot(n2, wpg_ref[...]))
    o_ref[...] = h + pg * _dot(p_ref[...].astype(BF16), wple_ref[...])


def _tail(x2, ya2, yf2, yb2, r2, kt2, v2, g2, gate2, p2, w, tm):
    m = x2.shape[0]
    const = lambda i: (0, 0)
    row = lambda i: (i, 0)
    rw = pl.BlockSpec((tm, WIDTH), row)
    vec = lambda c: pl.BlockSpec((1, c), const)
    once = lambda shape: pl.BlockSpec(shape, const, pipeline_mode=pl.Buffered(1))
    return pl.pallas_call(
        functools.partial(_tail_kernel, ff_chunk=512),
        grid=(m // tm,),
        in_specs=[
            pl.BlockSpec((tm, D_MODEL), row), rw, rw, rw, rw, rw, rw, rw,
            pl.BlockSpec((tm, 2 * D_MODEL), row),
            pl.BlockSpec((tm, PLE_DIM), row),
            vec(WIDTH), vec(WIDTH), vec(WIDTH),
            once((WIDTH, D_MODEL)), once((WIDTH, D_MODEL)), once((D_MODEL, D_MODEL)),
            vec(D_MODEL), once((D_MODEL, D_FF)), once((D_FF, D_MODEL)),
            vec(D_MODEL), once((D_MODEL, D_MODEL)), once((PLE_DIM, D_MODEL)),
        ],
        out_specs=pl.BlockSpec((tm, D_MODEL), row),
        out_shape=jax.ShapeDtypeStruct((m, D_MODEL), F32),
        compiler_params=pltpu.CompilerParams(
            dimension_semantics=("arbitrary",), vmem_limit_bytes=VMEM_LIMIT),
        name="tail",
    )(x2, ya2, yf2, yb2, r2, kt2, v2, g2, gate2, p2,
      w["lnw"], w["lnb"], w["rk"], w["wa"], w["wb"], w["wo"],
      w["g_ffn"], w["w1"], w["w2"], w["g_ple"], w["wpg"], w["wple"])


def _prepare_weights(g_mix, w_in, conv_w, q_gain, k_gain, rel_bias, w0_f, w_up_f, w0_b, w_up_b,
                     a0, a_up, g_up, k_k, k_a, r_k, ln_x_w, ln_x_b, w_a_out, w_b_out, w_o,
                     g_ffn, w_ff1, w_ff2, w_ple, g_ple, w_pgate):
    na = 3 * WIDTH
    rw0 = na
    sm0 = rw0 + 3 * WIDTH
    gate0 = sm0 + 2 * DECAY_RANK + AAA_RANK + GATE_RANK

    def regroup_small(t):
        lead = t[..., 0:2 * DECAY_RANK + AAA_RANK]
        pad = jnp.zeros(t.shape[:-1] + (128 - AAA_RANK,), t.dtype)
        return jnp.concatenate([lead, pad, t[..., 2 * DECAY_RANK + AAA_RANK:]], axis=-1)

    w_in_p = jnp.concatenate(
        [w_in[:, 0:sm0], regroup_small(w_in[:, sm0:gate0]), w_in[:, gate0:]], axis=1).astype(BF16)
    conv_rw = conv_w[:, 0:3 * WIDTH]
    conv_sm = regroup_small(conv_w[:, 3 * WIDTH:])
    zeros_up = jnp.zeros((DECAY_RANK, WIDTH), F32)
    w_up = jnp.concatenate([jnp.concatenate([w_up_f, zeros_up], axis=1),
                            jnp.concatenate([zeros_up, w_up_b], axis=1)], axis=0).astype(BF16)
    w0 = jnp.concatenate([w0_f, w0_b])[None, :]
    a_up_p = jnp.concatenate([a_up, jnp.zeros((128 - AAA_RANK, WIDTH), F32)], axis=0).astype(BF16)
    row = lambda t: t.reshape(1, -1)
    return dict(
        g_mix=row(g_mix), w_in=w_in_p, qg=row(jnp.tile(q_gain, HEADS)), kg=row(jnp.tile(k_gain, HEADS)),
        bias=_na_bias_table(rel_bias),
        prep=(conv_rw, conv_sm, w_up, w0, a_up_p, row(a0), g_up.astype(BF16), row(k_k), row(k_a)),
        lnw=row(ln_x_w), lnb=row(ln_x_b), rk=row(r_k),
        wa=w_a_out.astype(BF16), wb=w_b_out.astype(BF16), wo=w_o.astype(BF16),
        g_ffn=row(g_ffn), w1=w_ff1.astype(BF16), w2=w_ff2.astype(BF16),
        g_ple=row(g_ple), wpg=w_pgate.astype(BF16), wple=w_ple.astype(BF16),
    )


def _layer(x, p, w):
    b, l, _ = x.shape
    m = b * l
    tm = 512
    x2 = x.reshape(m, D_MODEL)
    qkv, gate, r, kt, v, al, be, g, lwf, lwb = _inproj(
        x2, l, w["g_mix"], w["w_in"], w["qg"], w["kg"], w["prep"], tm)
    y_a = _natten(qkv.reshape(b, l, 3 * WIDTH), w["bias"])
    seq = lambda t: t.reshape(b, l, WIDTH)
    y_f, y_b = _rwscan(seq(r), seq(kt), seq(v), seq(al), seq(be), seq(lwf), seq(lwb))
    flat = lambda t: t.reshape(m, WIDTH)
    out = _tail(x2, flat(y_a), flat(y_f), flat(y_b), r, kt, v, g, gate, p.reshape(m, PLE_DIM), w, tm)
    return out.reshape(b, l, D_MODEL)


def kernel(x_prompt, x_sample, p_prompt, p_sample, g_mix, w_in, conv_w, q_gain, k_gain, rel_bias,
           w0_f, w_up_f, w0_b, w_up_b, a0, a_up, g_up, k_k, k_a, r_k, ln_x_w, ln_x_b,
           w_a_out, w_b_out, w_o, g_ffn, w_ff1, w_ff2, w_ple, g_ple, w_pgate):
    params = (g_mix, w_in, conv_w, q_gain, k_gain, rel_bias, w0_f, w_up_f, w0_b, w_up_b, a0, a_up, g_up,
              k_k, k_a, r_k, ln_x_w, ln_x_b, w_a_out, w_b_out, w_o, g_ffn, w_ff1, w_ff2, w_ple, g_ple, w_pgate)
    depth = g_mix.shape[0]

    def run(h, p):
        for i in range(depth):
            h = _layer(h, p[i], _prepare_weights(*[t[i] for t in params]))
        return h

    return (run(x_prompt, p_prompt), run(x_sample, p_sample))
```

```python
import functools

import numpy as np
import jax
import jax.numpy as jnp
from jax import lax
from jax.experimental import pallas as pl
from jax.experimental.pallas import tpu as pltpu

F32 = jnp.float32
BF16 = jnp.bfloat16

D_MODEL = 1024
PLE_DIM = 256
GRID_W = 64
WIN_R = 8
WIN_C = 16
HEADS = 8
HEAD_DIM = 64
WIDTH = HEADS * HEAD_DIM
DECAY_RANK = 64
AAA_RANK = 64
GATE_RANK = 128
D_FF = 4 * D_MODEL
EPS = 1e-6
GN_EPS = 64e-5
DECAY_SCALE = 0.606531
NEG_INF = -1e30
LOG2E = 1.4426950408889634

SMALL_COLS = 384
HALO = 16
CHUNK = 64
SCAN_CHUNKS = 8
SCAN_STAGGER = 4
NA_ROWS_PER_STEP = 16
NA_ROW_GROUP = 8
VMEM_LIMIT = 56 * 1024 * 1024


def _dot(a, b):
    return jnp.dot(a, b, preferred_element_type=F32)


def _dot_nt(a, b):
    return lax.dot_general(a, b, (((1,), (1,)), ((), ())), preferred_element_type=F32)


def _dot_tn(a, b):
    return lax.dot_general(a, b, (((0,), (0,)), ((), ())), preferred_element_type=F32)


def _seg_sum(x):
    low =lax.broadcasted_iota(jnp.int32, (x.shape[0], 2 * HEAD_DIM), 1) < HEAD_DIM
    outs = []
    for c0 in range(0, x.shape[1], 2 * HEAD_DIM):
        slab = x[:, c0:c0 + 2 * HEAD_DIM]
        lo = jnp.sum(jnp.where(low, slab, 0.0), axis=-1, keepdims=True)
        hi = jnp.sum(jnp.where(low, 0.0, slab), axis=-1, keepdims=True)
        outs.append(jnp.where(low, lo, hi))
    return jnp.concatenate(outs, axis=1)


def _rms(x, gain):
    return x * lax.rsqrt(jnp.mean(x * x, axis=-1, keepdims=True) + EPS) * gain


def _inproj_kernel(x_ref, xp_ref, xn_ref, gmix_ref, w_ref, wsm_ref, wgate_ref, qg_ref, kg_ref,
                   cw_rkv_ref, cw_sm_ref, wup_ref, w0_ref, aup_ref, a0_ref, gup_ref, kk_ref, ka_ref,
                   qkv_ref, gate_ref, r_out, kt_out, v_out, al_out, be_out, g_out, lwf_out, lwb_out,
                   *, tm, tiles_per_seq):
    pos = pl.program_id(0) % tiles_per_seq
    gmix = gmix_ref[...]
    n = _rms(x_ref[...], gmix).astype(BF16)
    n_prev = jnp.where(pos > 0, _rms(xp_ref[...], gmix), 0.0).astype(BF16)
    n_next = jnp.where(pos < tiles_per_seq - 1, _rms(xn_ref[...], gmix), 0.0).astype(BF16)
    n_ext = jnp.concatenate([n_prev, n, n_next], axis=0)

    def head_rms(t, gain):
        ss = _seg_sum(t * t)
        return t * lax.rsqrt(ss * (1.0 / HEAD_DIM) + EPS) * gain

    q = head_rms(_dot(n, w_ref[:, 0:WIDTH]), qg_ref[...]) * (HEAD_DIM ** -0.5 * LOG2E)
    qkv_ref[:, 0:WIDTH] = q.astype(BF16)
    k = head_rms(_dot(n, w_ref[:, WIDTH:2 * WIDTH]), kg_ref[...])
    qkv_ref[:, WIDTH:2 * WIDTH] = k.astype(BF16)
    qkv_ref[:, 2 * WIDTH:3 * WIDTH] = _dot(n, w_ref[:, 2 * WIDTH:3 * WIDTH]).astype(BF16)
    gate_ref[...] = jax.nn.sigmoid(_dot(n, wgate_ref[...])).astype(BF16)

    ext = tm + 2 * HALO

    def conv3(z, cw):
        zp = pltpu.roll(z, 1, 0)[HALO:HALO + tm]
        zn = pltpu.roll(z, ext - 1, 0)[HALO:HALO + tm]
        return zp * cw[0:1, :] + z[HALO:HALO + tm] * cw[1:2, :] + zn * cw[2:3, :]

    u = conv3(_dot(n_ext, w_ref[:, 3 * WIDTH:6 * WIDTH]), cw_rkv_ref[...])
    us = conv3(_dot(n_ext, wsm_ref[...]), cw_sm_ref[...])
    kx = u[:, WIDTH:2 * WIDTH]
    pre = _dot(jnp.tanh(us[:, 0:128]).astype(BF16), wup_ref[...]) + w0_ref[...]
    logw = -DECAY_SCALE * jax.nn.sigmoid(pre)
    a = jax.nn.sigmoid(_dot(us[:, 128:256].astype(BF16), aup_ref[...]) + a0_ref[...])
    kk = kx * kk_ref[...]
    kk = kk * lax.rsqrt(jnp.maximum(_seg_sum(kk * kk), 1e-24))
    r_out[...] = u[:, 0:WIDTH].astype(BF16)
    kt_out[...] = (kx * (1.0 + (a - 1.0) * ka_ref[...])).astype(BF16)
    v_out[...] = u[:, 2 * WIDTH:3 * WIDTH].astype(BF16)
    al_out[...] = (-kk).astype(BF16)
    be_out[...] = (kk * a).astype(BF16)
    g_out[...] = _dot(jax.nn.sigmoid(us[:, 256:384]).astype(BF16), gup_ref[...]).astype(BF16)
    lwf_out[...] = logw[:, 0:WIDTH]
    lwb_out[...] = logw[:, WIDTH:2 * WIDTH]


def _inproj(x2, seq_len, gmix, w_in_p, qg, kg, prep_w, tm):
    m = x2.shape[0]
    assert seq_len % tm == 0 and tm % HALO == 0
    nh = m // HALO
    per = tm // HALO
    const = lambda i: (0, 0)
    row = lambda i: (i, 0)
    prev = lambda i: (jnp.maximum(i * per - 1, 0), 0)
    nxt = lambda i: (jnp.minimum((i + 1) * per, nh - 1), 0)
    vec = lambda c: pl.BlockSpec((1, c), const)
    wide = pl.BlockSpec((tm, WIDTH), row)
    out = lambda c, dt: jax.ShapeDtypeStruct((m, c), dt)
    return pl.pallas_call(
        functools.partial(_inproj_kernel, tm=tm, tiles_per_seq=seq_len // tm),
        grid=(m // tm,),
        in_specs=[
            pl.BlockSpec((tm, D_MODEL), row),
            pl.BlockSpec((HALO, D_MODEL), prev),
            pl.BlockSpec((HALO, D_MODEL), nxt),
            vec(D_MODEL),
            pl.BlockSpec((D_MODEL, 6 * WIDTH), const, pipeline_mode=pl.Buffered(1)),
            pl.BlockSpec((D_MODEL, SMALL_COLS), const, pipeline_mode=pl.Buffered(1)),
            pl.BlockSpec((D_MODEL, 2 * D_MODEL), const, pipeline_mode=pl.Buffered(1)),
            vec(WIDTH), vec(WIDTH),
            pl.BlockSpec((3, 3 * WIDTH), const), pl.BlockSpec((3, SMALL_COLS), const),
            pl.BlockSpec((128, 2 * WIDTH), const), vec(2 * WIDTH),
            pl.BlockSpec((128, WIDTH), const), vec(WIDTH),
            pl.BlockSpec((128, WIDTH), const),
            vec(WIDTH), vec(WIDTH),
        ],
        out_specs=[pl.BlockSpec((tm, 3 * WIDTH), row), pl.BlockSpec((tm, 2 * D_MODEL), row)] + [wide] * 8,
        out_shape=[out(3 * WIDTH, BF16), out(2 * D_MODEL, BF16)] + [out(WIDTH, BF16)] * 6 + [out(WIDTH, F32)] * 2,
        compiler_params=pltpu.CompilerParams(
            dimension_semantics=("arbitrary",), vmem_limit_bytes=VMEM_LIMIT),
        name="inproj",
    )(x2, x2, x2, gmix, *w_in_p, qg, kg, *prep_w)


def _natten_kernel(q_ref, k_ref, v_ref, bias_ref, o_ref, *, rows):
    r0 = pl.program_id(2) * NA_ROWS_PER_STEP
    win = WIN_R * GRID_W

    def body(gi, carry):
        low = lax.broadcasted_iota(jnp.int32, (GRID_W, 2 * HEAD_DIM), 1) < HEAD_DIM
        qs, ks, vs, offs, qstarts = [], [], [], [], []
        for t in range(NA_ROW_GROUP):
            i = gi * NA_ROW_GROUP + t
            r = r0 + i
            row_start = jnp.clip(r - WIN_R // 2, 0, rows - WIN_R)
            kstart = pl.multiple_of(row_start * GRID_W, GRID_W)
            qstart = pl.multiple_of(i * GRID_W, GRID_W)
            qrow = q_ref[0, pl.ds(qstart, GRID_W), :]
            qs.append(jnp.concatenate([jnp.where(low, qrow, 0), jnp.where(low, 0, qrow)], axis=0).astype(BF16))
            ks.append(k_ref[0, pl.ds(kstart, win), :])
            vs.append(v_ref[0, pl.ds(kstart, win), :])
            offs.append(r - row_start)
            qstarts.append(qstart)
        ids = range(NA_ROW_GROUP)
        s = [_dot_nt(qs[c], ks[c]) + bias_ref[0, offs[c]] for c in ids]
        e = [jnp.exp2(s[c] - jnp.max(s[c], axis=-1, keepdims=True)) for c in ids]
        l = [jnp.sum(e[c], axis=-1, keepdims=True) for c in ids]
        o = [_dot(e[c].astype(BF16), vs[c]) / l[c] for c in ids]
        for c in ids:
            o_ref[0, pl.ds(qstarts[c], GRID_W), :] = jnp.where(
                low, o[c][0:GRID_W], o[c][GRID_W:2 * GRID_W]).astype(BF16)
        return carry

    lax.fori_loop(0, NA_ROWS_PER_STEP // NA_ROW_GROUP, body, 0)


def _natten(qkv3, bias_tab):
    b, l, _ = qkv3.shape
    rows = l // GRID_W
    assert rows >= WIN_R and rows % NA_ROWS_PER_STEP == 0
    tq = NA_ROWS_PER_STEP * GRID_W
    npair = HEADS // 2
    return pl.pallas_call(
        functools.partial(_natten_kernel, rows=rows),
        grid=(b, npair, rows // NA_ROWS_PER_STEP),
        in_specs=[
            pl.BlockSpec((1, tq, 128), lambda bi, hp, ri: (bi, ri, hp)),
            pl.BlockSpec((1, l, 128), lambda bi, hp, ri: (bi, 0, npair + hp)),
            pl.BlockSpec((1, l, 128), lambda bi, hp, ri: (bi, 0, 2 * npair + hp)),
            pl.BlockSpec((1, WIN_R, 2 * GRID_W, WIN_R * GRID_W), lambda bi, hp, ri: (hp, 0, 0, 0)),
        ],
        out_specs=pl.BlockSpec((1, tq, 128), lambda bi, hp, ri: (bi, ri, hp)),
        out_shape=jax.ShapeDtypeStruct((b, l, WIDTH), BF16),
        compiler_params=pltpu.CompilerParams(
            dimension_semantics=("arbitrary", "arbitrary", "arbitrary"), vmem_limit_bytes=VMEM_LIMIT),
        name="natten",
    )(qkv3, qkv3, qkv3, bias_tab)


def _na_bias_table(rel_bias):
    qc = np.arange(GRID_W)[:, None]
    kc = np.arange(GRID_W)[None, :]
    dc = np.clip(kc - qc + WIN_C - 1, 0, 2 * WIN_C - 2)
    col_start = np.clip(qc - WIN_C // 2, 0, GRID_W - WIN_C)
    in_win = (kc >= col_start) & (kc < col_start + WIN_C)
    onehot = jnp.asarray(dc[:, :, None] == np.arange(2 * WIN_C - 1), F32)
    per_dr = jnp.einsum("hrc,qkc->hrqk", rel_bias.astype(F32), onehot, precision=lax.Precision.HIGHEST)
    per_dr = jnp.where(in_win[None, None], per_dr * LOG2E, NEG_INF)
    return jnp.stack(
        [per_dr[:, WIN_R - 1 - off:2 * WIN_R - 1 - off].transpose(0, 2, 1, 3)
         .reshape(HEADS // 2, 2 * GRID_W, WIN_R * GRID_W) for off in range(WIN_R)], axis=1)


def _scan_prepare(refs, rows, reverse):
    r_ref, k_ref, v_ref, al_ref, be_ref, lw_ref = refs
    c = CHUNK
    lw = lw_ref[0, rows, :]
    row = lax.broadcasted_iota(jnp.int32, lw.shape, 0)
    g = lw
    for shift in (1, 2, 4, 8, 16, 32):
        if reverse:
            g = g + jnp.where(row < c - shift, pltpu.roll(g, c - shift, 0), 0.0)
        else:
            g = g + jnp.where(row >= shift, pltpu.roll(g, shift, 0), 0.0)
    gtot = g[0:1, :] if reverse else g[c - 1:c, :]
    eng = jnp.exp(-g)
    egt = jnp.exp(gtot - g)
    be = be_ref[0, rows, :].astype(F32)
    kt = k_ref[0, rows, :].astype(F32)
    return dict(
        a=al_ref[0, rows, :].astype(F32) * jnp.exp(g - lw), r=r_ref[0, rows, :].astype(F32) * jnp.exp(g),
        b=(be * eng).astype(BF16), k=(kt * eng).astype(BF16), bh=be * egt, kh=kt * egt,
        v=v_ref[0, rows, :].astype(F32), dec_tot=jnp.exp(gtot))


def _rwscan_kernel(rf, kf, vf, af, bf, lwf, rb, kb, vb, ab, bb, lwb, mask_ref, yf_ref, yb_ref, s_ref):
    @pl.when(pl.program_id(1) == 0)
    def _():
        s_ref[...] = jnp.zeros_like(s_ref)

    c, hd, npair = CHUNK, HEAD_DIM, HEADS // 2
    refs = ((rf, kf, vf, af, bf, lwf), (rb, kb, vb, ab, bb, lwb))
    y_refs = (yf_ref, yb_ref)
    rows = lambda sub: slice(sub * c, (sub + 1) * c)
    keep = (mask_ref[0] != 0.0, mask_ref[1] != 0.0)
    pair = lambda t, hp: t[:, hp * 2 * hd:(hp + 1) * 2 * hd]
    low = lax.broadcasted_iota(jnp.int32, (c, 2 * hd), 1) < hd
    swap = lambda t: pltpu.roll(t, hd, 1)

    def block_rows(t):
        return jnp.concatenate([jnp.where(low, t, 0.0), jnp.where(low, 0.0, t)], axis=0)

    state = {(d, hp): s_ref[d, hp] for d in range(2) for hp in range(npair)}

    def chunk_stages(nth):
        sub_of = (nth, SCAN_CHUNKS - 1 - nth)
        prep = [_scan_prepare(refs[d], rows(sub_of[d]), d == 1) for d in range(2)]
        chains = [(d, hp) for d in range(2) for hp in range(npair)]
        ids = range(len(chains))
        xa = [pair(prep[d]["a"], hp) for d, hp in chains]
        lhs = [jnp.concatenate([block_rows(xa[i]), block_rows(pair(prep[d]["r"], hp))], axis=0).astype(BF16)
               for i, (d, hp) in enumerate(chains)]
        rhs = [jnp.concatenate([pair(prep[d]["b"], hp)] * 2 + [pair(prep[d]["k"], hp)] * 2, axis=0)
               for d, hp in chains]
        vcb = [jnp.concatenate([jnp.where(low, t, 0.0), jnp.where(low, swap(t), 0.0)], axis=0).astype(BF16)
               for t in (pair(prep[d]["v"], hp) for d, hp in chains)]
        yield
        sc = [jnp.where(keep[d], _dot_nt(lhs[i], rhs[i]), 0.0) for i, (d, hp) in enumerate(chains)]
        yield
        z = [jnp.concatenate([jnp.where(low, 0.0, swap(xa[i])), jnp.where(low, 0.0, xa[i])], axis=0)
             + _dot(sc[i][0:2 * c, 2 * c:4 * c].astype(BF16), vcb[i]) for i in ids]
        p = [sc[i][0:2 * c, 0:2 * c] for i in ids]
        yield
        for step in range(6):
            if step < 5:
                w = [_dot(p[i].astype(BF16), jnp.concatenate([z[i], p[i]], axis=1).astype(BF16)) for i in ids]
                z = [z[i] + w[i][:, 0:2 * hd] for i in ids]
                p = [w[i][:, 2 * hd:] for i in ids]
            else:
                z = [z[i] + _dot(p[i].astype(BF16), z[i].astype(BF16)) for i in ids]
            yield
        ar = [jnp.concatenate([jnp.where(low, swap(z[i][0:c]), 0.0).astype(BF16),
                               jnp.where(low, 0.0, z[i][c:2 * c]).astype(BF16), lhs[i][2 * c:4 * c]], axis=0)
              for i in ids]
        yh = [jnp.concatenate([block_rows(pair(prep[d]["bh"], hp)), block_rows(pair(prep[d]["kh"], hp))],
                              axis=0).astype(BF16) for d, hp in chains]
        yield
        ars = [_dot_nt(ar[i], state[chains[i]].astype(BF16)) for i in ids]
        yield
        uv = [jnp.concatenate([(ars[i][0:2 * c] + z[i][:, 0:hd]).astype(BF16), vcb[i][:, 0:hd]], axis=0)
              for i in ids]
        out = [ars[i][2 * c:4 * c] + _dot(sc[i][2 * c:4 * c, :].astype(BF16), uv[i]) for i in ids]
        for i, (d, hp) in enumerate(chains):
            state[d, hp] = state[d, hp] * pair(prep[d]["dec_tot"], hp) + _dot_tn(uv[i], yh[i])
        yield
        for d in range(2):
            y = [jnp.concatenate([out[i][0:c], out[i][c:2 * c]], axis=1)
                 for i in ids if chains[i][0] == d]
            y_refs[d][0, rows(sub_of[d]), :] = jnp.concatenate(y, axis=1)

    gens = [chunk_stages(nth) for nth in range(SCAN_CHUNKS)]
    started, live, rounds = 0, [], 0
    while started < SCAN_CHUNKS or live:
        if started < SCAN_CHUNKS and rounds % SCAN_STAGGER == 0:
            live.append(gens[started])
            started += 1
        for gen in list(live):
            try:
                next(gen)
            except StopIteration:
                live.remove(gen)
        rounds += 1
    for (d, hp), s_new in state.items():
        s_ref[d, hp] = s_new


def _scan_masks():
    idx = np.arange(4 * CHUNK)
    head = (idx % (2 * CHUNK)) // CHUNK
    tok = idx % CHUNK
    a_row = idx < 2 * CHUNK
    lag = tok[:, None] - tok[None, :]
    same = head[:, None] == head[None, :]
    need = np.where(a_row, 1, 0)[:, None]
    return jnp.asarray(np.stack([same & (lag >= need), same & (-lag >= need)]), F32)


def _rwscan(r, kt, v, al, be, lwf, lwb):
    b, l, _ = r.shape
    step = SCAN_CHUNKS * CHUNK
    assert l % step == 0
    nc = l // step
    fwd = lambda bi, n: (bi, n, 0)
    bwd = lambda bi, n: (bi, nc - 1 - n, 0)
    spec = lambda m: pl.BlockSpec((1, step, WIDTH), m)
    out = jax.ShapeDtypeStruct((b, l, WIDTH), F32)
    return pl.pallas_call(
        _rwscan_kernel,
        grid=(b, nc),
        in_specs=[spec(fwd)] * 6 + [spec(bwd)] * 6
        + [pl.BlockSpec((2, 4 * CHUNK, 4 * CHUNK), lambda bi, n: (0, 0, 0))],
        out_specs=[spec(fwd), spec(bwd)],
        out_shape=[out, out],
        scratch_shapes=[pltpu.VMEM((2, HEADS // 2, HEAD_DIM, 2 * HEAD_DIM), F32)],
        compiler_params=pltpu.CompilerParams(
            dimension_semantics=("arbitrary", "arbitrary"), vmem_limit_bytes=VMEM_LIMIT),
        name="rwscan",
    )(r, kt, v, al, be, lwf, r, kt, v, al, be, lwb, _scan_masks())


def _tail_kernel(x_ref, ya_ref, yf_ref, yb_ref, r_ref, kt_ref, v_ref, g_ref, gate_ref, p_ref,
                 lnw_ref, lnb_ref, rk_ref, wa_ref, wb_ref, wo_ref,
                 gffn_ref, w1_ref, w2_ref, gple_ref, wpg_ref, wple_ref, o_ref, *, ff_chunk):
    o = yf_ref[...] + yb_ref[...]
    mu = _seg_sum(o) * (1.0 / HEAD_DIM)
    dlt = o - mu
    var = _seg_sum(dlt * dlt) * (1.0 / HEAD_DIM)
    o = dlt * lax.rsqrt(var + GN_EPS) * lnw_ref[...] + lnb_ref[...]
    rk = r_ref[...].astype(F32) * kt_ref[...].astype(F32) * rk_ref[...]
    bonus = _seg_sum(rk) * v_ref[...].astype(F32)
    y_rw = ((o + bonus) * g_ref[...].astype(F32)).astype(BF16)
    merged = (gate_ref[:, 0:D_MODEL].astype(F32) * _dot(ya_ref[...], wa_ref[...])
              + gate_ref[:, D_MODEL:2 * D_MODEL].astype(F32) * _dot(y_rw, wb_ref[...]))
    h = x_ref[...] + _dot(merged.astype(BF16), wo_ref[...])
    n = _rms(h, gffn_ref[...]).astype(BF16)
    for c0 in range(0, D_FF, ff_chunk):
        f = jnp.square(jnp.maximum(_dot(n, w1_ref[:, c0:c0 + ff_chunk]), 0.0))
        h = h + _dot(f.astype(BF16), w2_ref[c0:c0 + ff_chunk, :])
    n2 = _rms(h, gple_ref[...]).astype(BF16)
    pg = jax.nn.sigmoid(_dot(n2, wpg_ref[...]))
    o_ref[...] = h + pg * _dot(p_ref[...].astype(BF16), wple_ref[...])


def _tail(x2, ya2, yf2, yb2, r2, kt2, v2, g2, gate2, p2, w, tm):
    m = x2.shape[0]
    const = lambda i: (0, 0)
    row = lambda i: (i, 0)
    rw = pl.BlockSpec((tm, WIDTH), row)
    vec = lambda c: pl.BlockSpec((1, c), const)
    once = lambda shape: pl.BlockSpec(shape, const, pipeline_mode=pl.Buffered(1))
    return pl.pallas_call(
        functools.partial(_tail_kernel, ff_chunk=512),
        grid=(m // tm,),
        in_specs=[
            pl.BlockSpec((tm, D_MODEL), row), rw, rw, rw, rw, rw, rw, rw,
            pl.BlockSpec((tm, 2 * D_MODEL), row),
            pl.BlockSpec((tm, PLE_DIM), row),
            vec(WIDTH), vec(WIDTH), vec(WIDTH),
            once((WIDTH, D_MODEL)), once((WIDTH, D_MODEL)), once((D_MODEL, D_MODEL)),
            vec(D_MODEL), once((D_MODEL, D_FF)), once((D_FF, D_MODEL)),
            vec(D_MODEL), once((D_MODEL, D_MODEL)), once((PLE_DIM, D_MODEL)),
        ],
        out_specs=pl.BlockSpec((tm, D_MODEL), row),
        out_shape=jax.ShapeDtypeStruct((m, D_MODEL), F32),
        compiler_params=pltpu.CompilerParams(
            dimension_semantics=("arbitrary",), vmem_limit_bytes=VMEM_LIMIT),
        name="tail",
    )(x2, ya2, yf2, yb2, r2, kt2, v2, g2, gate2, p2,
      w["lnw"], w["lnb"], w["rk"], w["wa"], w["wb"], w["wo"],
      w["g_ffn"], w["w1"], w["w2"], w["g_ple"], w["wpg"], w["wple"])


def _prepare_weights(g_mix, w_in, conv_w, q_gain, k_gain, rel_bias, w0_f, w_up_f, w0_b, w_up_b,
                     a0, a_up, g_up, k_k, k_a, r_k, ln_x_w, ln_x_b, w_a_out, w_b_out, w_o,
                     g_ffn, w_ff1, w_ff2, w_ple, g_ple, w_pgate):
    na = 3 * WIDTH
    rw0 = na
    sm0 = rw0 + 3 * WIDTH
    gate0 = sm0 + 2 * DECAY_RANK + AAA_RANK + GATE_RANK

    def regroup_small(t):
        lead = t[..., 0:2 * DECAY_RANK + AAA_RANK]
        pad = jnp.zeros(t.shape[:-1] + (128 - AAA_RANK,), t.dtype)
        return jnp.concatenate([lead, pad, t[..., 2 * DECAY_RANK + AAA_RANK:]], axis=-1)

    w_in_p = (w_in[:, 0:sm0].astype(BF16), regroup_small(w_in[:, sm0:gate0]).astype(BF16),
              w_in[:, gate0:].astype(BF16))
    conv_rw = conv_w[:, 0:3 * WIDTH]
    conv_sm = regroup_small(conv_w[:, 3 * WIDTH:])
    zeros_up = jnp.zeros((DECAY_RANK, WIDTH), F32)
    w_up = jnp.concatenate([jnp.concatenate([w_up_f, zeros_up], axis=1),
                            jnp.concatenate([zeros_up, w_up_b], axis=1)], axis=0).astype(BF16)
    w0 = jnp.concatenate([w0_f, w0_b])[None, :]
    a_up_p = jnp.concatenate([a_up, jnp.zeros((128 - AAA_RANK, WIDTH), F32)], axis=0).astype(BF16)
    row = lambda t: t.reshape(1, -1)
    return dict(
        g_mix=row(g_mix), w_in=w_in_p, qg=row(jnp.tile(q_gain, HEADS)), kg=row(jnp.tile(k_gain, HEADS)),
        bias=_na_bias_table(rel_bias),
        prep=(conv_rw, conv_sm, w_up, w0, a_up_p, row(a0), g_up.astype(BF16), row(k_k), row(k_a)),
        lnw=row(ln_x_w), lnb=row(ln_x_b), rk=row(r_k),
        wa=w_a_out.astype(BF16), wb=w_b_out.astype(BF16), wo=w_o.astype(BF16),
        g_ffn=row(g_ffn), w1=w_ff1.astype(BF16), w2=w_ff2.astype(BF16),
        g_ple=row(g_ple), wpg=w_pgate.astype(BF16), wple=w_ple.astype(BF16),
    )


def _layer(x, p, w):
    b, l, _ = x.shape
    m = b * l
    tm = 512
    x2 = x.reshape(m, D_MODEL)
    qkv, gate, r, kt, v, al, be, g, lwf, lwb = _inproj(
        x2, l, w["g_mix"], w["w_in"], w["qg"], w["kg"], w["prep"], tm)
    y_a = _natten(qkv.reshape(b, l, 3 * WIDTH), w["bias"])
    seq = lambda t: t.reshape(b, l, WIDTH)
    y_f, y_b = _rwscan(seq(r), seq(kt), seq(v), seq(al), seq(be), seq(lwf), seq(lwb))
    flat = lambda t: t.reshape(m, WIDTH)
    out = _tail(x2, flat(y_a), flat(y_f), flat(y_b), r, kt, v, g, gate, p.reshape(m, PLE_DIM), w, tm)
    return out.reshape(b, l, D_MODEL)


def kernel(x_prompt, x_sample, p_prompt, p_sample, g_mix, w_in, conv_w, q_gain, k_gain, rel_bias,
           w0_f, w_up_f, w0_b, w_up_b, a0, a_up, g_up, k_k, k_a, r_k, ln_x_w, ln_x_b,
           w_a_out, w_b_out, w_o, g_ffn, w_ff1, w_ff2, w_ple, g_ple, w_pgate):
    params = (g_mix, w_in, conv_w, q_gain, k_gain, rel_bias, w0_f, w_up_f, w0_b, w_up_b, a0, a_up, g_up,
              k_k, k_a, r_k, ln_x_w, ln_x_b, w_a_out, w_b_out, w_o, g_ffn, w_ff1, w_ff2, w_ple, g_ple, w_pgate)
    depth = g_mix.shape[0]

    def run(h, p):
        for i in range(depth):
            h = _layer(h, p[i], _prepare_weights(*[t[i] for t in params]))
        return h

    return (run(x_prompt, p_prompt), run(x_sample, p_sample))
```

```python
import functools

import numpy as np
import jax
import jax.numpy as jnp
from jax import lax
from jax.experimental import pallas as pl
from jax.experimental.pallas import tpu as pltpu

F32 = jnp.float32
BF16 = jnp.bfloat16

D_MODEL = 1024
PLE_DIM = 256
GRID_W = 64
WIN_R = 8
WIN_C = 16
HEADS = 8
HEAD_DIM = 64
WIDTH = HEADS * HEAD_DIM
DECAY_RANK = 64
AAA_RANK = 64
GATE_RANK = 128
D_FF = 4 * D_MODEL
EPS = 1e-6
GN_EPS = 64e-5
DECAY_SCALE = 0.606531
NEG_INF = -1e30
LOG2E = 1.4426950408889634

SMALL_COLS = 384
HALO = 16
CHUNK = 64
SCAN_CHUNKS = 8
SCAN_STAGGER = 4
NA_ROWS_PER_STEP = 32
NA_ROW_GROUP = 8
VMEM_LIMIT = 56 * 1024 * 1024


def _dot(a, b):
    return jnp.dot(a, b, preferred_element_type=F32)


def _dot_nt(a, b):
    return lax.dot_general(a, b, (((1,), (1,)), ((), ())), preferred_element_type=F32)


def _dot_tn(a, b):
    return lax.dot_general(a, b, (((0,), (0,)), ((), ())), preferred_element_type=F32)


def _seg_sum(x):
    low =lax.broadcasted_iota(jnp.int32, (x.shape[0], 2 * HEAD_DIM), 1) < HEAD_DIM
    outs = []
    for c0 in range(0, x.shape[1], 2 * HEAD_DIM):
        slab = x[:, c0:c0 + 2 * HEAD_DIM]
        lo = jnp.sum(jnp.where(low, slab, 0.0), axis=-1, keepdims=True)
        hi = jnp.sum(jnp.where(low, 0.0, slab), axis=-1, keepdims=True)
        outs.append(jnp.where(low, lo, hi))
    return jnp.concatenate(outs, axis=1)


def _rms(x, gain):
    return x * lax.rsqrt(jnp.mean(x * x, axis=-1, keepdims=True) + EPS) * gain


def _inproj_kernel(x_ref, xp_ref, xn_ref, gmix_ref, w_ref, wsm_ref, wgate_ref, qg_ref, kg_ref,
                   cw_rkv_ref, cw_sm_ref, wup_ref, w0_ref, aup_ref, a0_ref, gup_ref, kk_ref, ka_ref,
                   qkv_ref, gate_ref, r_out, kt_out, v_out, al_out, be_out, g_out, lwf_out, lwb_out,
                   *, tm, tiles_per_seq):
    pos = pl.program_id(0) % tiles_per_seq
    gmix = gmix_ref[...]
    n = _rms(x_ref[...], gmix).astype(BF16)
    n_prev = jnp.where(pos > 0, _rms(xp_ref[...], gmix), 0.0).astype(BF16)
    n_next = jnp.where(pos < tiles_per_seq - 1, _rms(xn_ref[...], gmix), 0.0).astype(BF16)
    n_ext = jnp.concatenate([n_prev, n, n_next], axis=0)

    def head_rms(t, gain):
        ss = _seg_sum(t * t)
        return t * lax.rsqrt(ss * (1.0 / HEAD_DIM) + EPS) * gain

    q = head_rms(_dot(n, w_ref[:, 0:WIDTH]), qg_ref[...]) * (HEAD_DIM ** -0.5 * LOG2E)
    qkv_ref[:, 0:WIDTH] = q.astype(BF16)
    k = head_rms(_dot(n, w_ref[:, WIDTH:2 * WIDTH]), kg_ref[...])
    qkv_ref[:, WIDTH:2 * WIDTH] = k.astype(BF16)
    qkv_ref[:, 2 * WIDTH:3 * WIDTH] = _dot(n, w_ref[:, 2 * WIDTH:3 * WIDTH]).astype(BF16)
    gate_ref[...] = jax.nn.sigmoid(_dot(n, wgate_ref[...])).astype(BF16)

    ext = tm + 2 * HALO

    def conv3(z, cw):
        zp = pltpu.roll(z, 1, 0)[HALO:HALO + tm]
        zn = pltpu.roll(z, ext - 1, 0)[HALO:HALO + tm]
        return zp * cw[0:1, :] + z[HALO:HALO + tm] * cw[1:2, :] + zn * cw[2:3, :]

    u = conv3(_dot(n_ext, w_ref[:, 3 * WIDTH:6 * WIDTH]), cw_rkv_ref[...])
    us = conv3(_dot(n_ext, wsm_ref[...]), cw_sm_ref[...])
    kx = u[:, WIDTH:2 * WIDTH]
    pre = _dot(jnp.tanh(us[:, 0:128]).astype(BF16), wup_ref[...]) + w0_ref[...]
    logw = -DECAY_SCALE * jax.nn.sigmoid(pre)
    a = jax.nn.sigmoid(_dot(us[:, 128:256].astype(BF16), aup_ref[...]) + a0_ref[...])
    kk = kx * kk_ref[...]
    kk = kk * lax.rsqrt(jnp.maximum(_seg_sum(kk * kk), 1e-24))
    r_out[...] = u[:, 0:WIDTH].astype(BF16)
    kt_out[...] = (kx * (1.0 + (a - 1.0) * ka_ref[...])).astype(BF16)
    v_out[...] = u[:, 2 * WIDTH:3 * WIDTH].astype(BF16)
    al_out[...] = (-kk).astype(BF16)
    be_out[...] = (kk * a).astype(BF16)
    g_out[...] = _dot(jax.nn.sigmoid(us[:, 256:384]).astype(BF16), gup_ref[...]).astype(BF16)
    lwf_out[...] = logw[:, 0:WIDTH]
    lwb_out[...] = logw[:, WIDTH:2 * WIDTH]


def _inproj(x2, seq_len, gmix, w_in_p, qg, kg, prep_w, tm):
    m = x2.shape[0]
    assert seq_len % tm == 0 and tm % HALO == 0
    nh = m // HALO
    per = tm // HALO
    const = lambda i: (0, 0)
    row = lambda i: (i, 0)
    prev = lambda i: (jnp.maximum(i * per - 1, 0), 0)
    nxt = lambda i: (jnp.minimum((i + 1) * per, nh - 1), 0)
    vec = lambda c: pl.BlockSpec((1, c), const)
    wide = pl.BlockSpec((tm, WIDTH), row)
    out = lambda c, dt: jax.ShapeDtypeStruct((m, c), dt)
    return pl.pallas_call(
        functools.partial(_inproj_kernel, tm=tm, tiles_per_seq=seq_len // tm),
        grid=(m // tm,),
        in_specs=[
            pl.BlockSpec((tm, D_MODEL), row),
            pl.BlockSpec((HALO, D_MODEL), prev),
            pl.BlockSpec((HALO, D_MODEL), nxt),
            vec(D_MODEL),
            pl.BlockSpec((D_MODEL, 6 * WIDTH), const, pipeline_mode=pl.Buffered(1)),
            pl.BlockSpec((D_MODEL, SMALL_COLS), const, pipeline_mode=pl.Buffered(1)),
            pl.BlockSpec((D_MODEL, 2 * D_MODEL), const, pipeline_mode=pl.Buffered(1)),
            vec(WIDTH), vec(WIDTH),
            pl.BlockSpec((3, 3 * WIDTH), const), pl.BlockSpec((3, SMALL_COLS), const),
            pl.BlockSpec((128, 2 * WIDTH), const), vec(2 * WIDTH),
            pl.BlockSpec((128, WIDTH), const), vec(WIDTH),
            pl.BlockSpec((128, WIDTH), const),
            vec(WIDTH), vec(WIDTH),
        ],
        out_specs=[pl.BlockSpec((tm, 3 * WIDTH), row), pl.BlockSpec((tm, 2 * D_MODEL), row)] + [wide] * 8,
        out_shape=[out(3 * WIDTH, BF16), out(2 * D_MODEL, BF16)] + [out(WIDTH, BF16)] * 6 + [out(WIDTH, F32)] * 2,
        compiler_params=pltpu.CompilerParams(
            dimension_semantics=("arbitrary",), vmem_limit_bytes=VMEM_LIMIT),
        name="inproj",
    )(x2, x2, x2, gmix, *w_in_p, qg, kg, *prep_w)


def _natten_kernel(q_ref, k_ref, v_ref, bias_ref, o_ref, *, rows):
    r0 = pl.program_id(2) * NA_ROWS_PER_STEP
    win = WIN_R * GRID_W

    def body(gi, carry):
        low = lax.broadcasted_iota(jnp.int32, (GRID_W, 2 * HEAD_DIM), 1) < HEAD_DIM
        qs, ks, vs, offs, qstarts = [], [], [], [], []
        for t in range(NA_ROW_GROUP):
            i = gi * NA_ROW_GROUP + t
            r = r0 + i
            row_start = jnp.clip(r - WIN_R // 2, 0, rows - WIN_R)
            kstart = pl.multiple_of(row_start * GRID_W, GRID_W)
            qstart = pl.multiple_of(i * GRID_W, GRID_W)
            qrow = q_ref[0, pl.ds(qstart, GRID_W), :]
            qs.append(jnp.concatenate([jnp.where(low, qrow, 0), jnp.where(low, 0, qrow)], axis=0).astype(BF16))
            ks.append(k_ref[0, pl.ds(kstart, win), :])
            vs.append(v_ref[0, pl.ds(kstart, win), :])
            offs.append(r - row_start)
            qstarts.append(qstart)
        ids = range(NA_ROW_GROUP)
        s = [_dot_nt(qs[c], ks[c]) + bias_ref[0, offs[c]] for c in ids]
        e = [jnp.exp2(s[c] - jnp.max(s[c], axis=-1, keepdims=True)) for c in ids]
        l = [jnp.sum(e[c], axis=-1, keepdims=True) for c in ids]
        o = [_dot(e[c].astype(BF16), vs[c]) / l[c] for c in ids]
        for c in ids:
            o_ref[0, pl.ds(qstarts[c], GRID_W), :] = jnp.where(
                low, o[c][0:GRID_W], o[c][GRID_W:2 * GRID_W]).astype(BF16)
        return carry

    lax.fori_loop(0, NA_ROWS_PER_STEP // NA_ROW_GROUP, body, 0)


def _natten(qkv3, bias_tab):
    b, l, _ = qkv3.shape
    rows = l // GRID_W
    assert rows >= WIN_R and rows % NA_ROWS_PER_STEP == 0
    tq = NA_ROWS_PER_STEP * GRID_W
    npair = HEADS // 2
    return pl.pallas_call(
        functools.partial(_natten_kernel, rows=rows),
        grid=(b, npair, rows // NA_ROWS_PER_STEP),
        in_specs=[
            pl.BlockSpec((1, tq, 128), lambda bi, hp, ri: (bi, ri, hp)),
            pl.BlockSpec((1, l, 128), lambda bi, hp, ri: (bi, 0, npair + hp)),
            pl.BlockSpec((1, l, 128), lambda bi, hp, ri: (bi, 0, 2 * npair + hp)),
            pl.BlockSpec((1, WIN_R, 2 * GRID_W, WIN_R * GRID_W), lambda bi, hp, ri: (hp, 0, 0, 0)),
        ],
        out_specs=pl.BlockSpec((1, tq, 128), lambda bi, hp, ri: (bi, ri, hp)),
        out_shape=jax.ShapeDtypeStruct((b, l, WIDTH), BF16),
        compiler_params=pltpu.CompilerParams(
            dimension_semantics=("arbitrary", "arbitrary", "arbitrary"), vmem_limit_bytes=VMEM_LIMIT),
        name="natten",
    )(qkv3, qkv3, qkv3, bias_tab)


def _na_bias_table(rel_bias):
    qc = np.arange(GRID_W)[:, None]
    kc = np.arange(GRID_W)[None, :]
    dc = np.clip(kc - qc + WIN_C - 1, 0, 2 * WIN_C - 2)
    col_start = np.clip(qc - WIN_C // 2, 0, GRID_W - WIN_C)
    in_win = (kc >= col_start) & (kc < col_start + WIN_C)
    onehot = jnp.asarray(dc[:, :, None] == np.arange(2 * WIN_C - 1), F32)
    per_dr = jnp.einsum("hrc,qkc->hrqk", rel_bias.astype(F32), onehot, precision=lax.Precision.HIGHEST)
    per_dr = jnp.where(in_win[None, None], per_dr * LOG2E, NEG_INF)
    return jnp.stack(
        [per_dr[:, WIN_R - 1 - off:2 * WIN_R - 1 - off].transpose(0, 2, 1, 3)
         .reshape(HEADS // 2, 2 * GRID_W, WIN_R * GRID_W) for off in range(WIN_R)], axis=1)


def _scan_prepare(refs, rows, reverse):
    r_ref, k_ref, v_ref, al_ref, be_ref, lw_ref = refs
    c = CHUNK
    lw = lw_ref[0, rows, :]
    row = lax.broadcasted_iota(jnp.int32, lw.shape, 0)
    g = lw
    for shift in (1, 2, 4, 8, 16, 32):
        if reverse:
            g = g + jnp.where(row < c - shift, pltpu.roll(g, c - shift, 0), 0.0)
        else:
            g = g + jnp.where(row >= shift, pltpu.roll(g, shift, 0), 0.0)
    gtot = g[0:1, :] if reverse else g[c - 1:c, :]
    eng = jnp.exp(-g)
    egt = jnp.exp(gtot - g)
    be = be_ref[0, rows, :].astype(F32)
    kt = k_ref[0, rows, :].astype(F32)
    return dict(
        a=al_ref[0, rows, :].astype(F32) * jnp.exp(g - lw), r=r_ref[0, rows, :].astype(F32) * jnp.exp(g),
        b=(be * eng).astype(BF16), k=(kt * eng).astype(BF16), bh=be * egt, kh=kt * egt,
        v=v_ref[0, rows, :].astype(F32), dec_tot=jnp.exp(gtot))


def _rwscan_kernel(rf, kf, vf, af, bf, lwf, rb, kb, vb, ab, bb, lwb, mask_ref, yf_ref, yb_ref, s_ref):
    @pl.when(pl.program_id(1) == 0)
    def _():
        s_ref[...] = jnp.zeros_like(s_ref)

    c, hd, npair = CHUNK, HEAD_DIM, HEADS // 2
    refs = ((rf, kf, vf, af, bf, lwf), (rb, kb, vb, ab, bb, lwb))
    y_refs = (yf_ref, yb_ref)
    rows = lambda sub: slice(sub * c, (sub + 1) * c)
    keep = (mask_ref[0] != 0.0, mask_ref[1] != 0.0)
    pair = lambda t, hp: t[:, hp * 2 * hd:(hp + 1) * 2 * hd]
    low = lax.broadcasted_iota(jnp.int32, (c, 2 * hd), 1) < hd
    swap = lambda t: pltpu.roll(t, hd, 1)

    def block_rows(t):
        return jnp.concatenate([jnp.where(low, t, 0.0), jnp.where(low, 0.0, t)], axis=0)

    state = {(d, hp): s_ref[d, hp] for d in range(2) for hp in range(npair)}

    def chunk_stages(nth):
        sub_of = (nth, SCAN_CHUNKS - 1 - nth)
        prep = [_scan_prepare(refs[d], rows(sub_of[d]), d == 1) for d in range(2)]
        chains = [(d, hp) for d in range(2) for hp in range(npair)]
        ids = range(len(chains))
        xa = [pair(prep[d]["a"], hp) for d, hp in chains]
        lhs = [jnp.concatenate([block_rows(xa[i]), block_rows(pair(prep[d]["r"], hp))], axis=0).astype(BF16)
               for i, (d, hp) in enumerate(chains)]
        rhs = [jnp.concatenate([pair(prep[d]["b"], hp), pair(prep[d]["k"], hp)], axis=0)
               for d, hp in chains]
        vcb = [jnp.concatenate([jnp.where(low, t, 0.0), jnp.where(low, swap(t), 0.0)], axis=0).astype(BF16)
               for t in (pair(prep[d]["v"], hp) for d, hp in chains)]
        yield
        st = [jnp.where(keep[d], _dot_nt(lhs[i], rhs[i]), 0.0) for i, (d, hp) in enumerate(chains)]
        sw = [swap(t) for t in st]

        def head_diag(i, r0, first):
            own, other = (st[i], sw[i]) if first else (sw[i], st[i])
            return jnp.concatenate([jnp.where(low, own[r0:r0 + c], 0.0),
                                    jnp.where(low, 0.0, other[r0 + c:r0 + 2 * c])], axis=0)

        p = [head_diag(i, 0, True) for i in ids]
        a_k = [head_diag(i, 0, False).astype(BF16) for i in ids]
        a_r = [jnp.concatenate([head_diag(i, 2 * c, True), head_diag(i, 2 * c, False)], axis=1).astype(BF16)
               for i in ids]
        yield
        z = [jnp.concatenate([jnp.where(low, 0.0, swap(xa[i])), jnp.where(low, 0.0, xa[i])], axis=0)
             + _dot(a_k[i], vcb[i]) for i in ids]
        yield
        for step in range(6):
            if step < 5:
                w = [_dot(p[i].astype(BF16), jnp.concatenate([z[i], p[i]], axis=1).astype(BF16)) for i in ids]
                z = [z[i] + w[i][:, 0:2 * hd] for i in ids]
                p = [w[i][:, 2 * hd:] for i in ids]
            else:
                z = [z[i] + _dot(p[i].astype(BF16), z[i].astype(BF16)) for i in ids]
            yield
        ar = [jnp.concatenate([jnp.where(low, swap(z[i][0:c]), 0.0).astype(BF16),
                               jnp.where(low, 0.0, z[i][c:2 * c]).astype(BF16), lhs[i][2 * c:4 * c]], axis=0)
              for i in ids]
        yh = [jnp.concatenate([block_rows(pair(prep[d]["bh"], hp)), block_rows(pair(prep[d]["kh"], hp))],
                              axis=0).astype(BF16) for d, hp in chains]
        yield
        ars = [_dot_nt(ar[i], state[chains[i]].astype(BF16)) for i in ids]
        yield
        uv = [jnp.concatenate([(ars[i][0:2 * c] + z[i][:, 0:hd]).astype(BF16), vcb[i][:, 0:hd]], axis=0)
              for i in ids]
        out = [ars[i][2 * c:4 * c] + _dot(a_r[i], uv[i]) for i in ids]
        for i, (d, hp) in enumerate(chains):
            state[d, hp] = state[d, hp] * pair(prep[d]["dec_tot"], hp) + _dot_tn(uv[i], yh[i])
        yield
        for d in range(2):
            y = [jnp.concatenate([out[i][0:c], out[i][c:2 * c]], axis=1)
                 for i in ids if chains[i][0] == d]
            y_refs[d][0, rows(sub_of[d]), :] = jnp.concatenate(y, axis=1)

    gens = [chunk_stages(nth) for nth in range(SCAN_CHUNKS)]
    started, live, rounds = 0, [], 0
    while started < SCAN_CHUNKS or live:
        if started < SCAN_CHUNKS and rounds % SCAN_STAGGER == 0:
            live.append(gens[started])
            started += 1
        for gen in list(live):
            try:
                next(gen)
            except StopIteration:
                live.remove(gen)
        rounds += 1
    for (d, hp), s_new in state.items():
        s_ref[d, hp] = s_new


def _scan_masks():
    tok_r = np.arange(4 * CHUNK) % CHUNK
    tok_c = np.arange(2 * CHUNK) % CHUNK
    need = np.where(np.arange(4 * CHUNK) < 2 * CHUNK, 1, 0)[:, None]
    lag = tok_r[:, None] - tok_c[None, :]
    return jnp.asarray(np.stack([lag >= need, -lag >= need]), F32)


def _rwscan(r, kt, v, al, be, lwf, lwb):
    b, l, _ = r.shape
    step = SCAN_CHUNKS * CHUNK
    assert l % step == 0
    nc = l // step
    fwd = lambda bi, n: (bi, n, 0)
    bwd = lambda bi, n: (bi, nc - 1 - n, 0)
    spec = lambda m: pl.BlockSpec((1, step, WIDTH), m)
    out = jax.ShapeDtypeStruct((b, l, WIDTH), F32)
    return pl.pallas_call(
        _rwscan_kernel,
        grid=(b, nc),
        in_specs=[spec(fwd)] * 6 + [spec(bwd)] * 6
        + [pl.BlockSpec((2, 4 * CHUNK, 2 * CHUNK), lambda bi, n: (0, 0, 0))],
        out_specs=[spec(fwd), spec(bwd)],
        out_shape=[out, out],
        scratch_shapes=[pltpu.VMEM((2, HEADS // 2, HEAD_DIM, 2 * HEAD_DIM), F32)],
        compiler_params=pltpu.CompilerParams(
            dimension_semantics=("arbitrary", "arbitrary"), vmem_limit_bytes=VMEM_LIMIT),
        name="rwscan",
    )(r, kt, v, al, be, lwf, r, kt, v, al, be, lwb, _scan_masks())


def _tail_kernel(x_ref, ya_ref, yf_ref, yb_ref, r_ref, kt_ref, v_ref, g_ref, gate_ref, p_ref,
                 lnw_ref, lnb_ref, rk_ref, wa_ref, wb_ref, wo_ref,
                 gffn_ref, w1_ref, w2_ref, gple_ref, wpg_ref, wple_ref, o_ref, *, ff_chunk):
    o = yf_ref[...] + yb_ref[...]
    mu = _seg_sum(o) * (1.0 / HEAD_DIM)
    dlt = o - mu
    var = _seg_sum(dlt * dlt) * (1.0 / HEAD_DIM)
    o = dlt * lax.rsqrt(var + GN_EPS) * lnw_ref[...] + lnb_ref[...]
    rk = r_ref[...].astype(F32) * kt_ref[...].astype(F32) * rk_ref[...]
    bonus = _seg_sum(rk) * v_ref[...].astype(F32)
    y_rw = ((o + bonus) * g_ref[...].astype(F32)).astype(BF16)
    merged = (gate_ref[:, 0:D_MODEL].astype(F32) * _dot(ya_ref[...], wa_ref[...])
              + gate_ref[:, D_MODEL:2 * D_MODEL].astype(F32) * _dot(y_rw, wb_ref[...]))
    h = x_ref[...] + _dot(merged.astype(BF16), wo_ref[...])
    n = _rms(h, gffn_ref[...]).astype(BF16)
    for c0 in range(0, D_FF, ff_chunk):
        f = jnp.square(jnp.maximum(_dot(n, w1_ref[:, c0:c0 + ff_chunk]), 0.0))
        h = h + _dot(f.astype(BF16), w2_ref[c0:c0 + ff_chunk, :])
    n2 = _rms(h, gple_ref[...]).astype(BF16)
    pg = jax.nn.sigmoid(_dot(n2, wpg_ref[...]))
    o_ref[...] = h + pg * _dot(p_ref[...].astype(BF16), wple_ref[...])


def _tail(x2, ya2, yf2, yb2, r2, kt2, v2, g2, gate2, p2, w, tm):
    m = x2.shape[0]
    const = lambda i: (0, 0)
    row = lambda i: (i, 0)
    rw = pl.BlockSpec((tm, WIDTH), row)
    vec = lambda c: pl.BlockSpec((1, c), const)
    once = lambda shape: pl.BlockSpec(shape, const, pipeline_mode=pl.Buffered(1))
    return pl.pallas_call(
        functools.partial(_tail_kernel, ff_chunk=512),
        grid=(m // tm,),
        in_specs=[
            pl.BlockSpec((tm, D_MODEL), row), rw, rw, rw, rw, rw, rw, rw,
            pl.BlockSpec((tm, 2 * D_MODEL), row),
            pl.BlockSpec((tm, PLE_DIM), row),
            vec(WIDTH), vec(WIDTH), vec(WIDTH),
            once((WIDTH, D_MODEL)), once((WIDTH, D_MODEL)), once((D_MODEL, D_MODEL)),
            vec(D_MODEL), once((D_MODEL, D_FF)), once((D_FF, D_MODEL)),
            vec(D_MODEL), once((D_MODEL, D_MODEL)), once((PLE_DIM, D_MODEL)),
        ],
        out_specs=pl.BlockSpec((tm, D_MODEL), row),
        out_shape=jax.ShapeDtypeStruct((m, D_MODEL), F32),
        compiler_params=pltpu.CompilerParams(
            dimension_semantics=("arbitrary",), vmem_limit_bytes=VMEM_LIMIT),
        name="tail",
    )(x2, ya2, yf2, yb2, r2, kt2, v2, g2, gate2, p2,
      w["lnw"], w["lnb"], w["rk"], w["wa"], w["wb"], w["wo"],
      w["g_ffn"], w["w1"], w["w2"], w["g_ple"], w["wpg"], w["wple"])


def _prepare_weights(g_mix, w_in, conv_w, q_gain, k_gain, rel_bias, w0_f, w_up_f, w0_b, w_up_b,
                     a0, a_up, g_up, k_k, k_a, r_k, ln_x_w, ln_x_b, w_a_out, w_b_out, w_o,
                     g_ffn, w_ff1, w_ff2, w_ple, g_ple, w_pgate):
    na = 3 * WIDTH
    rw0 = na
    sm0 = rw0 + 3 * WIDTH
    gate0 = sm0 + 2 * DECAY_RANK + AAA_RANK + GATE_RANK

    def regroup_small(t):
        lead = t[..., 0:2 * DECAY_RANK + AAA_RANK]
        pad = jnp.zeros(t.shape[:-1] + (128 - AAA_RANK,), t.dtype)
        return jnp.concatenate([lead, pad, t[..., 2 * DECAY_RANK + AAA_RANK:]], axis=-1)

    w_in_p = (w_in[:, 0:sm0].astype(BF16), regroup_small(w_in[:, sm0:gate0]).astype(BF16),
              w_in[:, gate0:].astype(BF16))
    conv_rw = conv_w[:, 0:3 * WIDTH]
    conv_sm = regroup_small(conv_w[:, 3 * WIDTH:])
    zeros_up = jnp.zeros((DECAY_RANK, WIDTH), F32)
    w_up = jnp.concatenate([jnp.concatenate([w_up_f, zeros_up], axis=1),
                            jnp.concatenate([zeros_up, w_up_b], axis=1)], axis=0).astype(BF16)
    w0 = jnp.concatenate([w0_f, w0_b])[None, :]
    a_up_p = jnp.concatenate([a_up, jnp.zeros((128 - AAA_RANK, WIDTH), F32)], axis=0).astype(BF16)
    row = lambda t: t.reshape(1, -1)
    return dict(
        g_mix=row(g_mix), w_in=w_in_p, qg=row(jnp.tile(q_gain, HEADS)), kg=row(jnp.tile(k_gain, HEADS)),
        bias=_na_bias_table(rel_bias),
        prep=(conv_rw, conv_sm, w_up, w0, a_up_p, row(a0), g_up.astype(BF16), row(k_k), row(k_a)),
        lnw=row(ln_x_w), lnb=row(ln_x_b), rk=row(r_k),
        wa=w_a_out.astype(BF16), wb=w_b_out.astype(BF16), wo=w_o.astype(BF16),
        g_ffn=row(g_ffn), w1=w_ff1.astype(BF16), w2=w_ff2.astype(BF16),
        g_ple=row(g_ple), wpg=w_pgate.astype(BF16), wple=w_ple.astype(BF16),
    )


def _layer(x, p, w):
    b, l, _ = x.shape
    m = b * l
    tm = 512
    x2 = x.reshape(m, D_MODEL)
    qkv, gate, r, kt, v, al, be, g, lwf, lwb = _inproj(
        x2, l, w["g_mix"], w["w_in"], w["qg"], w["kg"], w["prep"], tm)
    y_a = _natten(qkv.reshape(b, l, 3 * WIDTH), w["bias"])
    seq = lambda t: t.reshape(b, l, WIDTH)
    y_f, y_b = _rwscan(seq(r), seq(kt), seq(v), seq(al), seq(be), seq(lwf), seq(lwb))
    flat = lambda t: t.reshape(m, WIDTH)
    out = _tail(x2, flat(y_a), flat(y_f), flat(y_b), r, kt, v, g, gate, p.reshape(m, PLE_DIM), w, tm)
    return out.reshape(b, l, D_MODEL)


def kernel(x_prompt, x_sample, p_prompt, p_sample, g_mix, w_in, conv_w, q_gain, k_gain, rel_bias,
           w0_f, w_up_f, w0_b, w_up_b, a0, a_up, g_up, k_k, k_a, r_k, ln_x_w, ln_x_b,
           w_a_out, w_b_out, w_o, g_ffn, w_ff1, w_ff2, w_ple, g_ple, w_pgate):
    params = (g_mix, w_in, conv_w, q_gain, k_gain, rel_bias, w0_f, w_up_f, w0_b, w_up_b, a0, a_up, g_up,
              k_k, k_a, r_k, ln_x_w, ln_x_b, w_a_out, w_b_out, w_o, g_ffn, w_ff1, w_ff2, w_ple, g_ple, w_pgate)
    depth = g_mix.shape[0]

    def run(h, p):
        for i in range(depth):
            h = _layer(h, p[i], _prepare_weights(*[t[i] for t in params]))
        return h

    return (run(x_prompt, p_prompt), run(x_sample, p_sample))
```

```python
import functools

import numpy as np
import jax
import jax.numpy as jnp
from jax import lax
from jax.experimental import pallas as pl
from jax.experimental.pallas import tpu as pltpu

F32 = jnp.float32
BF16 = jnp.bfloat16

D_MODEL = 1024
PLE_DIM = 256
GRID_W = 64
WIN_R = 8
WIN_C = 16
HEADS = 8
HEAD_DIM = 64
WIDTH = HEADS * HEAD_DIM
DECAY_RANK = 64
AAA_RANK = 64
GATE_RANK = 128
D_FF = 4 * D_MODEL
EPS = 1e-6
GN_EPS = 64e-5
DECAY_SCALE = 0.606531
NEG_INF = -1e30
LOG2E = 1.4426950408889634

LANES = 128
SMALL_COLS = 3 * LANES
HALO = 16
CHUNK = 64
SCAN_CHUNKS = 8
SCAN_STAGGER = 4
NA_ROWS_PER_STEP = 32
NA_ROW_GROUP = 8
VMEM_LIMIT = 56 * 1024 * 1024


def _dot(a, b):
    return jnp.dot(a, b, preferred_element_type=F32)


def _dot_nt(a, b):
    return lax.dot_general(a, b, (((1,), (1,)), ((), ())), preferred_element_type=F32)


def _dot_tn(a, b):
    return lax.dot_general(a, b, (((0,), (0,)), ((), ())), preferred_element_type=F32)


def _seg_sum(x):
    low = lax.broadcasted_iota(jnp.int32, (x.shape[0], 2 * HEAD_DIM), 1) < HEAD_DIM
    outs = []
    for c0 in range(0, x.shape[1], 2 * HEAD_DIM):
        slab = x[:, c0:c0 + 2 * HEAD_DIM]
        lo = jnp.sum(jnp.where(low, slab, 0.0), axis=-1, keepdims=True)
        hi = jnp.sum(jnp.where(low, 0.0, slab), axis=-1, keepdims=True)
        outs.append(jnp.where(low, lo, hi))
    return jnp.concatenate(outs, axis=1)


def _rms(x, gain):
    return x * lax.rsqrt(jnp.mean(x * x, axis=-1, keepdims=True) + EPS) * gain


def _sigmoid(x):
    return 0.5 * jnp.tanh(0.5 * x) + 0.5


def _inproj_kernel(x_ref, xp_ref, xn_ref, gmix_ref, w_ref, wsm_ref, wgate_ref, qg_ref, kg_ref,
                   cw_rkv_ref, cw_sm_ref, wup_ref, w0_ref, aup_ref, a0_ref, gup_ref, kk_ref, ka_ref,
                   qkv_ref, gate_ref, r_out, kt_out, v_out, al_out, be_out, g_out, lwf_out, lwb_out,
                   *, tm, tiles_per_seq):
    pos = pl.program_id(0) % tiles_per_seq
    gmix = gmix_ref[...]
    n = _rms(x_ref[...], gmix).astype(BF16)
    n_prev = jnp.where(pos > 0, _rms(xp_ref[...], gmix), 0.0).astype(BF16)
    n_next = jnp.where(pos < tiles_per_seq - 1, _rms(xn_ref[...], gmix), 0.0).astype(BF16)
    n_ext = jnp.concatenate([n_prev, n, n_next], axis=0)

    def head_rms(t, gain):
        ss = _seg_sum(t * t)
        return t * lax.rsqrt(ss * (1.0 / HEAD_DIM) + EPS) * gain

    q = head_rms(_dot(n, w_ref[:, 0:WIDTH]), qg_ref[...])
    qkv_ref[:, 0:WIDTH] = q.astype(BF16)
    k = head_rms(_dot(n, w_ref[:, WIDTH:2 * WIDTH]), kg_ref[...])
    qkv_ref[:, WIDTH:2 * WIDTH] = k.astype(BF16)
    qkv_ref[:, 2 * WIDTH:3 * WIDTH] = _dot(n, w_ref[:, 2 * WIDTH:3 * WIDTH]).astype(BF16)
    gate_ref[...] = _sigmoid(_dot(n, wgate_ref[...])).astype(BF16)

    ext = tm + 2 * HALO

    def conv3(z, cw):
        zp = pltpu.roll(z, 1, 0)[HALO:HALO + tm]
        zn = pltpu.roll(z, ext - 1, 0)[HALO:HALO + tm]
        return zp * cw[0:1, :] + z[HALO:HALO + tm] * cw[1:2, :] + zn * cw[2:3, :]

    u = conv3(_dot(n_ext, w_ref[:, 3 * WIDTH:6 * WIDTH]), cw_rkv_ref[...])
    us = conv3(_dot(n_ext, wsm_ref[...]), cw_sm_ref[...])
    kx = u[:, WIDTH:2 * WIDTH]
    pre = _dot(jnp.tanh(us[:, 0:LANES]).astype(BF16), wup_ref[...]) + w0_ref[...]
    logw = -DECAY_SCALE * _sigmoid(pre)
    a = _sigmoid(_dot(us[:, LANES:2 * LANES].astype(BF16), aup_ref[...]) + a0_ref[...])
    kk = kx * kk_ref[...]
    kk = kk * lax.rsqrt(jnp.maximum(_seg_sum(kk * kk), 1e-24))
    r_out[...] = u[:, 0:WIDTH].astype(BF16)
    kt_out[...] = (kx * (1.0 + (a - 1.0) * ka_ref[...])).astype(BF16)
    v_out[...] = u[:, 2 * WIDTH:3 * WIDTH].astype(BF16)
    al_out[...] = (-kk).astype(BF16)
    be_out[...] = (kk * a).astype(BF16)
    g_out[...] = _dot(_sigmoid(us[:, 2 * LANES:3 * LANES]).astype(BF16), gup_ref[...]).astype(BF16)
    lwf_out[...] = logw[:, 0:WIDTH]
    lwb_out[...] = logw[:, WIDTH:2 * WIDTH]


def _inproj(x2, seq_len, gmix, w_in_p, qg, kg, prep_w, tm):
    m = x2.shape[0]
    assert seq_len % tm == 0 and tm % HALO == 0
    nh = m // HALO
    per = tm // HALO
    const = lambda i: (0, 0)
    row = lambda i: (i, 0)
    prev = lambda i: (jnp.maximum(i * per - 1, 0), 0)
    nxt = lambda i: (jnp.minimum((i + 1) * per, nh - 1), 0)
    vec = lambda c: pl.BlockSpec((1, c), const)
    wide = pl.BlockSpec((tm, WIDTH), row)
    out = lambda c, dt: jax.ShapeDtypeStruct((m, c), dt)
    return pl.pallas_call(
        functools.partial(_inproj_kernel, tm=tm, tiles_per_seq=seq_len // tm),
        grid=(m // tm,),
        in_specs=[
            pl.BlockSpec((tm, D_MODEL), row),
            pl.BlockSpec((HALO, D_MODEL), prev),
            pl.BlockSpec((HALO, D_MODEL), nxt),
            vec(D_MODEL),
            pl.BlockSpec((D_MODEL, 6 * WIDTH), const, pipeline_mode=pl.Buffered(1)),
            pl.BlockSpec((D_MODEL, SMALL_COLS), const, pipeline_mode=pl.Buffered(1)),
            pl.BlockSpec((D_MODEL, 2 * D_MODEL), const, pipeline_mode=pl.Buffered(1)),
            vec(WIDTH), vec(WIDTH),
            pl.BlockSpec((3, 3 * WIDTH), const), pl.BlockSpec((3, SMALL_COLS), const),
            pl.BlockSpec((LANES, 2 * WIDTH), const), vec(2 * WIDTH),
            pl.BlockSpec((LANES, WIDTH), const), vec(WIDTH),
            pl.BlockSpec((LANES, WIDTH), const),
            vec(WIDTH), vec(WIDTH),
        ],
        out_specs=[pl.BlockSpec((tm, 3 * WIDTH), row), pl.BlockSpec((tm, 2 * D_MODEL), row)] + [wide] * 8,
        out_shape=[out(3 * WIDTH, BF16), out(2 * D_MODEL, BF16)] + [out(WIDTH, BF16)] * 6 + [out(WIDTH, F32)] * 2,
        compiler_params=pltpu.CompilerParams(
            dimension_semantics=("arbitrary",), vmem_limit_bytes=VMEM_LIMIT),
        name="inproj",
    )(x2, x2, x2, gmix, *w_in_p, qg, kg, *prep_w)


def _natten_kernel(q_ref, k_ref, v_ref, bias_ref, o_ref, *, rows):
    r0 = pl.program_id(2) * NA_ROWS_PER_STEP
    win = WIN_R * GRID_W

    def body(gi, carry):
        low = lax.broadcasted_iota(jnp.int32, (GRID_W, 2 * HEAD_DIM), 1) < HEAD_DIM
        qs, ks, vs, offs, qstarts = [], [], [], [], []
        for t in range(NA_ROW_GROUP):
            i = gi * NA_ROW_GROUP + t
            r = r0 + i
            row_start = jnp.clip(r - WIN_R // 2, 0, rows - WIN_R)
            kstart = pl.multiple_of(row_start * GRID_W, GRID_W)
            qstart = pl.multiple_of(i * GRID_W, GRID_W)
            qrow = q_ref[0, pl.ds(qstart, GRID_W), :]
            qs.append(jnp.concatenate([jnp.where(low, qrow, 0), jnp.where(low, 0, qrow)], axis=0).astype(BF16))
            ks.append(k_ref[0, pl.ds(kstart, win), :])
            vs.append(v_ref[0, pl.ds(kstart, win), :])
            offs.append(r - row_start)
            qstarts.append(qstart)
        ids = range(NA_ROW_GROUP)
        s = [_dot_nt(qs[c], ks[c]) + bias_ref[0, offs[c]] for c in ids]
        e = [jnp.exp2(s[c] - jnp.max(s[c], axis=-1, keepdims=True)) for c in ids]
        l = [jnp.sum(e[c], axis=-1, keepdims=True) for c in ids]
        o = [_dot(e[c].astype(BF16), vs[c]) / l[c] for c in ids]
        for c in ids:
            o_ref[0, pl.ds(qstarts[c], GRID_W), :] = jnp.where(
                low, o[c][0:GRID_W], o[c][GRID_W:2 * GRID_W]).astype(BF16)
        return carry

    lax.fori_loop(0, NA_ROWS_PER_STEP // NA_ROW_GROUP, body, 0)


def _natten(qkv3, bias_tab):
    b, l, _ = qkv3.shape
    rows = l // GRID_W
    assert rows >= WIN_R and rows % NA_ROWS_PER_STEP == 0
    tq = NA_ROWS_PER_STEP * GRID_W
    npair = HEADS // 2
    return pl.pallas_call(
        functools.partial(_natten_kernel, rows=rows),
        grid=(b, npair, rows // NA_ROWS_PER_STEP),
        in_specs=[
            pl.BlockSpec((1, tq, LANES), lambda bi, hp, ri: (bi, ri, hp)),
            pl.BlockSpec((1, l, LANES), lambda bi, hp, ri: (bi, 0, npair + hp)),
            pl.BlockSpec((1, l, LANES), lambda bi, hp, ri: (bi, 0, 2 * npair + hp)),
            pl.BlockSpec((1, WIN_R, 2 * GRID_W, WIN_R * GRID_W), lambda bi, hp, ri: (hp, 0, 0, 0)),
        ],
        out_specs=pl.BlockSpec((1, tq, LANES), lambda bi, hp, ri: (bi, ri, hp)),
        out_shape=jax.ShapeDtypeStruct((b, l, WIDTH), BF16),
        compiler_params=pltpu.CompilerParams(
            dimension_semantics=("arbitrary", "arbitrary", "arbitrary"), vmem_limit_bytes=VMEM_LIMIT),
        name="natten",
    )(qkv3, qkv3, qkv3, bias_tab)


def _na_bias_table(rel_bias):
    qc = np.arange(GRID_W)[:, None]
    kc = np.arange(GRID_W)[None, :]
    dc = np.clip(kc - qc + WIN_C - 1, 0, 2 * WIN_C - 2)
    col_start = np.clip(qc - WIN_C // 2, 0, GRID_W - WIN_C)
    in_win = (kc >= col_start) & (kc < col_start + WIN_C)
    onehot = jnp.asarray(dc[:, :, None] == np.arange(2 * WIN_C - 1), F32)
    per_dr = jnp.einsum("hrc,qkc->hrqk", rel_bias.astype(F32), onehot, precision=lax.Precision.HIGHEST)
    per_dr = jnp.where(in_win[None, None], per_dr * LOG2E, NEG_INF)
    return jnp.stack(
        [per_dr[:, WIN_R - 1 - off:2 * WIN_R - 1 - off].transpose(0, 2, 1, 3)
         .reshape(HEADS // 2, 2 * GRID_W, WIN_R * GRID_W) for off in range(WIN_R)], axis=1)


def _scan_prepare(refs, rows, reverse):
    r_ref, k_ref, v_ref, al_ref, be_ref, lw_ref = refs
    c = CHUNK
    lw = lw_ref[0, rows, :]
    row = lax.broadcasted_iota(jnp.int32, lw.shape, 0)
    g = lw
    for shift in (1, 2, 4, 8, 16, 32):
        if reverse:
            g = g + jnp.where(row < c - shift, pltpu.roll(g, c - shift, 0), 0.0)
        else:
            g = g + jnp.where(row >= shift, pltpu.roll(g, shift, 0), 0.0)
    gtot = g[0:1, :] if reverse else g[c - 1:c, :]
    eng = jnp.exp(-g)
    egt = jnp.exp(gtot - g)
    be = be_ref[0, rows, :].astype(F32)
    kt = k_ref[0, rows, :].astype(F32)
    return dict(
        a=al_ref[0, rows, :].astype(F32) * jnp.exp(g - lw), r=r_ref[0, rows, :].astype(F32) * jnp.exp(g),
        b=(be * eng).astype(BF16), k=(kt * eng).astype(BF16), bh=be * egt, kh=kt * egt,
        v=v_ref[0, rows, :].astype(F32), dec_tot=jnp.exp(gtot))


def _rwscan_kernel(rf, kf, vf, af, bf, lwf, rb, kb, vb, ab, bb, lwb, mask_ref, yf_ref, yb_ref, s_ref):
    @pl.when(pl.program_id(1) == 0)
    def _():
        s_ref[...] = jnp.zeros_like(s_ref)

    c, hd, npair = CHUNK, HEAD_DIM, HEADS // 2
    refs = ((rf, kf, vf, af, bf, lwf), (rb, kb, vb, ab, bb, lwb))
    y_refs = (yf_ref, yb_ref)
    rows = lambda sub: slice(sub * c, (sub + 1) * c)
    keep = (mask_ref[0] != 0.0, mask_ref[1] != 0.0)
    pair = lambda t, hp: t[:, hp * 2 * hd:(hp + 1) * 2 * hd]
    low = lax.broadcasted_iota(jnp.int32, (c, 2 * hd), 1) < hd
    swap = lambda t: pltpu.roll(t, hd, 1)

    def block_rows(t):
        return jnp.concatenate([jnp.where(low, t, 0.0), jnp.where(low, 0.0, t)], axis=0)

    state = {(d, hp): s_ref[d, hp] for d in range(2) for hp in range(npair)}

    def chunk_stages(nth):
        sub_of = (nth, SCAN_CHUNKS - 1 - nth)
        prep = [_scan_prepare(refs[d], rows(sub_of[d]), d == 1) for d in range(2)]
        chains = [(d, hp) for d in range(2) for hp in range(npair)]
        ids = range(len(chains))
        xa = [pair(prep[d]["a"], hp) for d, hp in chains]
        lhs = [jnp.concatenate([block_rows(xa[i]), block_rows(pair(prep[d]["r"], hp))], axis=0).astype(BF16)
               for i, (d, hp) in enumerate(chains)]
        rhs = [jnp.concatenate([pair(prep[d]["b"], hp), pair(prep[d]["k"], hp)], axis=0)
               for d, hp in chains]
        vcb = [jnp.concatenate([jnp.where(low, t, 0.0), jnp.where(low, swap(t), 0.0)], axis=0).astype(BF16)
               for t in (pair(prep[d]["v"], hp) for d, hp in chains)]
        yield
        st = [jnp.where(keep[d], _dot_nt(lhs[i], rhs[i]), 0.0) for i, (d, hp) in enumerate(chains)]
        sw = [swap(t) for t in st]

        def head_diag(i, r0, first):
            own, other = (st[i], sw[i]) if first else (sw[i], st[i])
            return jnp.concatenate([jnp.where(low, own[r0:r0 + c], 0.0),
                                    jnp.where(low, 0.0, other[r0 + c:r0 + 2 * c])], axis=0)

        p = [head_diag(i, 0, True) for i in ids]
        a_k = [head_diag(i, 0, False).astype(BF16) for i in ids]
        a_r = [jnp.concatenate([head_diag(i, 2 * c, True), head_diag(i, 2 * c, False)], axis=1).astype(BF16)
               for i in ids]
        yield
        z = [jnp.concatenate([jnp.where(low, 0.0, swap(xa[i])), jnp.where(low, 0.0, xa[i])], axis=0)
             + _dot(a_k[i], vcb[i]) for i in ids]
        yield
        for step in range(6):
            if step < 5:
                w = [_dot(p[i].astype(BF16), jnp.concatenate([z[i], p[i]], axis=1).astype(BF16)) for i in ids]
                z = [z[i] + w[i][:, 0:2 * hd] for i in ids]
                p = [w[i][:, 2 * hd:] for i in ids]
            else:
                z = [z[i] + _dot(p[i].astype(BF16), z[i].astype(BF16)) for i in ids]
            yield
        ar = [jnp.concatenate([jnp.where(low, swap(z[i][0:c]), 0.0).astype(BF16),
                               jnp.where(low, 0.0, z[i][c:2 * c]).astype(BF16), lhs[i][2 * c:4 * c]], axis=0)
              for i in ids]
        yh = [jnp.concatenate([block_rows(pair(prep[d]["bh"], hp)), block_rows(pair(prep[d]["kh"], hp))],
                              axis=0).astype(BF16) for d, hp in chains]
        yield
        ars = [_dot_nt(ar[i], state[chains[i]].astype(BF16)) for i in ids]
        yield
        uv = [jnp.concatenate([(ars[i][0:2 * c] + z[i][:, 0:hd]).astype(BF16), vcb[i][:, 0:hd]], axis=0)
              for i in ids]
        out = [ars[i][2 * c:4 * c] + _dot(a_r[i], uv[i]) for i in ids]
        for i, (d, hp) in enumerate(chains):
            state[d, hp] = state[d, hp] * pair(prep[d]["dec_tot"], hp) + _dot_tn(uv[i], yh[i])
        yield
        for d in range(2):
            y = [jnp.concatenate([out[i][0:c], out[i][c:2 * c]], axis=1)
                 for i in ids if chains[i][0] == d]
            y_refs[d][0, rows(sub_of[d]), :] = jnp.concatenate(y, axis=1)

    gens = [chunk_stages(nth) for nth in range(SCAN_CHUNKS)]
    started, live, rounds = 0, [], 0
    while started < SCAN_CHUNKS or live:
        if started < SCAN_CHUNKS and rounds % SCAN_STAGGER == 0:
            live.append(gens[started])
            started += 1
        for gen in list(live):
            try:
                next(gen)
            except StopIteration:
                live.remove(gen)
        rounds += 1
    for (d, hp), s_new in state.items():
        s_ref[d, hp] = s_new


def _scan_masks():
    tok_r = np.arange(4 * CHUNK) % CHUNK
    tok_c = np.arange(2 * CHUNK) % CHUNK
    need = np.where(np.arange(4 * CHUNK) < 2 * CHUNK, 1, 0)[:, None]
    lag = tok_r[:, None] - tok_c[None, :]
    return jnp.asarray(np.stack([lag >= need, -lag >= need]), F32)


def _rwscan(r, kt, v, al, be, lwf, lwb):
    b, l, _ = r.shape
    step = SCAN_CHUNKS * CHUNK
    assert l % step == 0
    nc = l // step
    fwd = lambda bi, n: (bi, n, 0)
    bwd = lambda bi, n: (bi, nc - 1 - n, 0)
    spec = lambda m: pl.BlockSpec((1, step, WIDTH), m)
    out = jax.ShapeDtypeStruct((b, l, WIDTH), F32)
    return pl.pallas_call(
        _rwscan_kernel,
        grid=(b, nc),
        in_specs=[spec(fwd)] * 6 + [spec(bwd)] * 6
        + [pl.BlockSpec((2, 4 * CHUNK, 2 * CHUNK), lambda bi, n: (0, 0, 0))],
        out_specs=[spec(fwd), spec(bwd)],
        out_shape=[out, out],
        scratch_shapes=[pltpu.VMEM((2, HEADS // 2, HEAD_DIM, 2 * HEAD_DIM), F32)],
        compiler_params=pltpu.CompilerParams(
            dimension_semantics=("arbitrary", "arbitrary"), vmem_limit_bytes=VMEM_LIMIT),
        name="rwscan",
    )(r, kt, v, al, be, lwf, r, kt, v, al, be, lwb, _scan_masks())


def _tail_kernel(x_ref, ya_ref, yf_ref, yb_ref, r_ref, kt_ref, v_ref, g_ref, gate_ref, p_ref,
                 lnw_ref, lnb_ref, rk_ref, wa_ref, wb_ref, wo_ref,
                 gffn_ref, w1_ref, w2_ref, gple_ref, wpg_ref, wple_ref, o_ref, *, ff_chunk):
    o = yf_ref[...] + yb_ref[...]
    mu = _seg_sum(o) * (1.0 / HEAD_DIM)
    dlt = o - mu
    var = _seg_sum(dlt * dlt) * (1.0 / HEAD_DIM)
    o = dlt * lax.rsqrt(var + GN_EPS) * lnw_ref[...] + lnb_ref[...]
    rk = r_ref[...].astype(F32) * kt_ref[...].astype(F32) * rk_ref[...]
    bonus = _seg_sum(rk) * v_ref[...].astype(F32)
    y_rw = ((o + bonus) * g_ref[...].astype(F32)).astype(BF16)
    merged = (gate_ref[:, 0:D_MODEL].astype(F32) * _dot(ya_ref[...], wa_ref[...])
              + gate_ref[:, D_MODEL:2 * D_MODEL].astype(F32) * _dot(y_rw, wb_ref[...]))
    h = x_ref[...] + _dot(merged.astype(BF16), wo_ref[...])
    n = _rms(h, gffn_ref[...]).astype(BF16)
    for c0 in range(0, D_FF, ff_chunk):
        f = jnp.square(jnp.maximum(_dot(n, w1_ref[:, c0:c0 + ff_chunk]), 0.0))
        h = h + _dot(f.astype(BF16), w2_ref[c0:c0 + ff_chunk, :])
    n2 = _rms(h, gple_ref[...]).astype(BF16)
    pg = _sigmoid(_dot(n2, wpg_ref[...]))
    o_ref[...] = h + pg * _dot(p_ref[...].astype(BF16), wple_ref[...])


def _tail(x2, ya2, yf2, yb2, r2, kt2, v2, g2, gate2, p2, w, tm):
    m = x2.shape[0]
    const = lambda i: (0, 0)
    row = lambda i: (i, 0)
    rw = pl.BlockSpec((tm, WIDTH), row)
    vec = lambda c: pl.BlockSpec((1, c), const)
    once = lambda shape: pl.BlockSpec(shape, const, pipeline_mode=pl.Buffered(1))
    return pl.pallas_call(
        functools.partial(_tail_kernel, ff_chunk=512),
        grid=(m // tm,),
        in_specs=[
            pl.BlockSpec((tm, D_MODEL), row), rw, rw, rw, rw, rw, rw, rw,
            pl.BlockSpec((tm, 2 * D_MODEL), row),
            pl.BlockSpec((tm, PLE_DIM), row),
            vec(WIDTH), vec(WIDTH), vec(WIDTH),
            once((WIDTH, D_MODEL)), once((WIDTH, D_MODEL)), once((D_MODEL, D_MODEL)),
            vec(D_MODEL), once((D_MODEL, D_FF)), once((D_FF, D_MODEL)),
            vec(D_MODEL), once((D_MODEL, D_MODEL)), once((PLE_DIM, D_MODEL)),
        ],
        out_specs=pl.BlockSpec((tm, D_MODEL), row),
        out_shape=jax.ShapeDtypeStruct((m, D_MODEL), F32),
        compiler_params=pltpu.CompilerParams(
            dimension_semantics=("arbitrary",), vmem_limit_bytes=VMEM_LIMIT),
        name="tail",
    )(x2, ya2, yf2, yb2, r2, kt2, v2, g2, gate2, p2,
      w["lnw"], w["lnb"], w["rk"], w["wa"], w["wb"], w["wo"],
      w["g_ffn"], w["w1"], w["w2"], w["g_ple"], w["wpg"], w["wple"])


def _prepare_weights(g_mix, w_in, conv_w, q_gain, k_gain, rel_bias, w0_f, w_up_f, w0_b, w_up_b,
                     a0, a_up, g_up, k_k, k_a, r_k, ln_x_w, ln_x_b, w_a_out, w_b_out, w_o,
                     g_ffn, w_ff1, w_ff2, w_ple, g_ple, w_pgate):
    na = 3 * WIDTH
    rw0 = na
    sm0 = rw0 + 3 * WIDTH
    gate0 = sm0 + 2 * DECAY_RANK + AAA_RANK + GATE_RANK

    def regroup_small(t):
        lead = t[..., 0:2 * DECAY_RANK + AAA_RANK]
        pad = jnp.zeros(t.shape[:-1] + (LANES - AAA_RANK,), t.dtype)
        return jnp.concatenate([lead, pad, t[..., 2 * DECAY_RANK + AAA_RANK:]], axis=-1)

    w_in_p = (w_in[:, 0:sm0].astype(BF16), regroup_small(w_in[:, sm0:gate0]).astype(BF16),
              w_in[:, gate0:].astype(BF16))
    conv_rw = conv_w[:, 0:3 * WIDTH]
    conv_sm = regroup_small(conv_w[:, 3 * WIDTH:])
    zeros_up = jnp.zeros((DECAY_RANK, WIDTH), F32)
    w_up = jnp.concatenate([jnp.concatenate([w_up_f, zeros_up], axis=1),
                            jnp.concatenate([zeros_up, w_up_b], axis=1)], axis=0).astype(BF16)
    w0 = jnp.concatenate([w0_f, w0_b])[None, :]
    a_up_p = jnp.concatenate([a_up, jnp.zeros((LANES - AAA_RANK, WIDTH), F32)], axis=0).astype(BF16)
    row = lambda t: t.reshape(1, -1)
    return dict(
        g_mix=row(g_mix), w_in=w_in_p, kg=row(jnp.tile(k_gain, HEADS)),
        qg=row(jnp.tile(q_gain, HEADS)) * (HEAD_DIM ** -0.5 * LOG2E),
        bias=_na_bias_table(rel_bias),
        prep=(conv_rw, conv_sm, w_up, w0, a_up_p, row(a0), g_up.astype(BF16), row(k_k), row(k_a)),
        lnw=row(ln_x_w), lnb=row(ln_x_b), rk=row(r_k),
        wa=w_a_out.astype(BF16), wb=w_b_out.astype(BF16), wo=w_o.astype(BF16),
        g_ffn=row(g_ffn), w1=w_ff1.astype(BF16), w2=w_ff2.astype(BF16),
        g_ple=row(g_ple), wpg=w_pgate.astype(BF16), wple=w_ple.astype(BF16),
    )


def _layer(x, p, w):
    b, l, _ = x.shape
    m = b * l
    tm = 512
    x2 = x.reshape(m, D_MODEL)
    qkv, gate, r, kt, v, al, be, g, lwf, lwb = _inproj(
        x2, l, w["g_mix"], w["w_in"], w["qg"], w["kg"], w["prep"], tm)
    y_a = _natten(qkv.reshape(b, l, 3 * WIDTH), w["bias"])
    seq = lambda t: t.reshape(b, l, WIDTH)
    y_f, y_b = _rwscan(seq(r), seq(kt), seq(v), seq(al), seq(be), seq(lwf), seq(lwb))
    flat = lambda t: t.reshape(m, WIDTH)
    out = _tail(x2, flat(y_a), flat(y_f), flat(y_b), r, kt, v, g, gate, p.reshape(m, PLE_DIM), w, tm)
    return out.reshape(b, l, D_MODEL)


def kernel(x_prompt, x_sample, p_prompt, p_sample, g_mix, w_in, conv_w, q_gain, k_gain, rel_bias,
           w0_f, w_up_f, w0_b, w_up_b, a0, a_up, g_up, k_k, k_a, r_k, ln_x_w, ln_x_b,
           w_a_out, w_b_out, w_o, g_ffn, w_ff1, w_ff2, w_ple, g_ple, w_pgate):
    params = (g_mix, w_in, conv_w, q_gain, k_gain, rel_bias, w0_f, w_up_f, w0_b, w_up_b, a0, a_up, g_up,
              k_k, k_a, r_k, ln_x_w, ln_x_b, w_a_out, w_b_out, w_o, g_ffn, w_ff1, w_ff2, w_ple, g_ple, w_pgate)
    depth = g_mix.shape[0]

    def run(h, p):
        for i in range(depth):
            h = _layer(h, p[i], _prepare_weights(*[t[i] for t in params]))
        return h

    return (run(x_prompt, p_prompt), run(x_sample, p_sample))
```
